```python
import jax, jax.numpy as jnp
from jax import lax
import numpy as np

D_MODEL = 2048
BATCH = 4
SEQ = 8192
DEPTH = 1

PLE_DIM = 256
SB_HEADS = 8
SB_HEAD_DIM = 128
MLA_HEADS = 8
MLA_NOPE_DIM = 128
MLA_ROPE_DIM = 64
MLA_V_DIM = 128
MLA_Q_RANK = 512
MLA_KV_RANK = 512
D_FF = 4 * D_MODEL
BLOCK_Q = 128
ROPE_THETA = 10000.0
EPS = 1e-6
SB_WIDTH = SB_HEADS * SB_HEAD_DIM
MLA_WIDTH = MLA_HEADS * MLA_V_DIM
MLA_QK_DIM = MLA_NOPE_DIM + MLA_ROPE_DIM
IN_SPLITS = (SB_WIDTH, SB_WIDTH, SB_WIDTH, MLA_Q_RANK, MLA_KV_RANK, MLA_ROPE_DIM, D_MODEL, D_MODEL)
IN_WIDTH = sum(IN_SPLITS)

kernel_name = "hybrid_stickbreak_mla_gated_block"


def rmsnorm(x, g):
    xf = x.astype(jnp.float32)
    y = xf * lax.rsqrt(jnp.mean(xf * xf, axis=-1, keepdims=True) + EPS)
    return (y * g.astype(jnp.float32)).astype(x.dtype)


def rope(x, cos, sin):
    xf = x.astype(jnp.float32)
    x1, x2 = jnp.split(xf, 2, axis=-1)
    out = jnp.concatenate([x1 * cos - x2 * sin, x1 * sin + x2 * cos], axis=-1)
    return out.astype(x.dtype)


def to_blocks(t):
    b, s = t.shape[0], t.shape[1]
    return jnp.moveaxis(t.reshape(b, s // BLOCK_Q, BLOCK_Q, *t.shape[2:]), 1, 0)


def from_blocks(t):
    t = jnp.moveaxis(t, 0, 1)
    return t.reshape(t.shape[0], t.shape[1] * t.shape[2], *t.shape[3:])


def stick_breaking_attention(q, k, v):
    s_len = q.shape[1]
    scale = SB_HEAD_DIM ** -0.5
    kpos = jnp.arange(s_len)

    def block(args):
        qb, i = args
        qpos = i * BLOCK_Q + jnp.arange(BLOCK_Q)
        z = jnp.einsum('bqhd,bkhd->bhqk', qb, k, preferred_element_type=jnp.float32) * scale
        mask = kpos[None, :] < qpos[:, None]
        log_fail = jnp.where(mask, jax.nn.log_sigmoid(-z), 0.0)
        later = lax.cumsum(log_fail, axis=3, reverse=True) - log_fail
        w = jnp.where(mask, jnp.exp(jax.nn.log_sigmoid(z) + later), 0.0)
        return jnp.einsum('bhqk,bkhd->bqhd', w.astype(v.dtype), v)

    out = lax.map(block, (to_blocks(q), jnp.arange(s_len // BLOCK_Q)))
    return from_blocks(out)


def mla_attention(q_nope, q_rope, k_nope, k_rope, v):
    s_len = q_nope.shape[1]
    scale = MLA_QK_DIM ** -0.5
    kpos = jnp.arange(s_len)

    def block(args):
        qn, qr, i = args
        qpos = i * BLOCK_Q + jnp.arange(BLOCK_Q)
        s = (jnp.einsum('bqhd,bkhd->bhqk', qn, k_nope, preferred_element_type=jnp.float32)
             + jnp.einsum('bqhr,bkr->bhqk', qr, k_rope, preferred_element_type=jnp.float32)) * scale
        mask = kpos[None, :] <= qpos[:, None]
        pr = jax.nn.softmax(jnp.where(mask, s, -jnp.inf), axis=-1)
        return jnp.einsum('bhqk,bkhd->bqhd', pr.astype(v.dtype), v)

    out = lax.map(block, (to_blocks(q_nope), to_blocks(q_rope), jnp.arange(s_len // BLOCK_Q)))
    return from_blocks(out)


def hybrid_layer(x, p_i, cos, sin, g_pre_mix, w_in, g_cq, g_ckv, w_q_up, w_kv_up,
                 w_sb_o, w_mla_o, w_out, g_post_mix, g_pre_mlp, w_up, w_down,
                 g_post_mlp, w_ple, g_ple, w_ple_gate):
    b, s, _ = x.shape
    h = rmsnorm(x, g_pre_mix)
    proj = h @ w_in
    offsets = [int(o) for o in np.cumsum(IN_SPLITS)[:-1]]
    sb_q, sb_k, sb_v, c_q, c_kv, k_r, gate_sb, gate_mla = jnp.split(proj, offsets, axis=-1)

    o_sb = stick_breaking_attention(sb_q.reshape(b, s, SB_HEADS, SB_HEAD_DIM),
                                    sb_k.reshape(b, s, SB_HEADS, SB_HEAD_DIM),
                                    sb_v.reshape(b, s, SB_HEADS, SB_HEAD_DIM))
    o_sb = o_sb.reshape(b, s, SB_WIDTH) @ w_sb_o

    q = (rmsnorm(c_q, g_cq) @ w_q_up).reshape(b, s, MLA_HEADS, MLA_QK_DIM)
    q_nope = q[..., :MLA_NOPE_DIM]
    q_rope = rope(q[..., MLA_NOPE_DIM:], cos[:, :, None, :], sin[:, :, None, :])
    kv = (rmsnorm(c_kv, g_ckv) @ w_kv_up).reshape(b, s, MLA_HEADS, MLA_NOPE_DIM + MLA_V_DIM)
    k_nope, v = kv[..., :MLA_NOPE_DIM], kv[..., MLA_NOPE_DIM:]
    k_rope = rope(k_r, cos, sin)
    o_mla = mla_attention(q_nope, q_rope, k_nope, k_rope, v)
    o_mla = o_mla.reshape(b, s, MLA_WIDTH) @ w_mla_o

    mixed = jax.nn.sigmoid(gate_sb) * o_sb + jax.nn.sigmoid(gate_mla) * o_mla
    x = x + rmsnorm(mixed @ w_out, g_post_mix)

    h = rmsnorm(x, g_pre_mlp)
    u = jnp.square(jax.nn.relu(h @ w_up))
    x = x + rmsnorm(u @ w_down, g_post_mlp)

    e = rmsnorm(p_i @ w_ple, g_ple)
    x = x + jax.nn.sigmoid(x @ w_ple_gate) * e
    return x


def setup_inputs(seed: int = 0) -> dict:
    key = jax.random.key(seed)
    ks = jax.random.split(key, 24)
    f32 = jnp.float32

    def dense(k, fan_in, fan_out):
        return jax.random.normal(k, (DEPTH, fan_in, fan_out), f32) * fan_in ** -0.5

    def gain(k, n):
        return 1.0 + 0.02 * jax.random.normal(k, (DEPTH, n), f32)

    x = jax.random.normal(ks[0], (BATCH, SEQ, D_MODEL), f32)
    p = jax.random.normal(ks[1], (DEPTH, BATCH, SEQ, PLE_DIM), f32)
    offset = jax.random.randint(ks[2], (BATCH, 1), 0, 1024, dtype=jnp.int32)
    positions = (jnp.arange(SEQ, dtype=jnp.int32)[None, :] + offset).astype(jnp.int32)
    return {
        "x": x,
        "p": p,
        "positions": positions,
        "g_pre_mix": gain(ks[3], D_MODEL),
        "w_in": dense(ks[4], D_MODEL, IN_WIDTH),
        "g_cq": gain(ks[5], MLA_Q_RANK),
        "g_ckv": gain(ks[6], MLA_KV_RANK),
        "w_q_up": dense(ks[7], MLA_Q_RANK, MLA_HEADS * MLA_QK_DIM),
        "w_kv_up": dense(ks[8], MLA_KV_RANK, MLA_HEADS * (MLA_NOPE_DIM + MLA_V_DIM)),
        "w_sb_o": dense(ks[9], SB_WIDTH, D_MODEL),
        "w_mla_o": dense(ks[10], MLA_WIDTH, D_MODEL),
        "w_out": dense(ks[11], D_MODEL, D_MODEL),
        "g_post_mix": gain(ks[12], D_MODEL),
        "g_pre_mlp": gain(ks[13], D_MODEL),
        "w_up": dense(ks[14], D_MODEL, D_FF),
        "w_down": dense(ks[15], D_FF, D_MODEL),
        "g_post_mlp": gain(ks[16], D_MODEL),
        "w_ple": dense(ks[17], PLE_DIM, D_MODEL),
        "g_ple": gain(ks[18], D_MODEL),
        "w_ple_gate": dense(ks[19], D_MODEL, D_MODEL),
    }


def reference(x, p, positions, g_pre_mix, w_in, g_cq, g_ckv, w_q_up, w_kv_up,
              w_sb_o, w_mla_o, w_out, g_post_mix, g_pre_mlp, w_up, w_down,
              g_post_mlp, w_ple, g_ple, w_ple_gate):
    half = MLA_ROPE_DIM // 2
    inv_freq = ROPE_THETA ** (-jnp.arange(half, dtype=jnp.float32) / half)
    ang = positions.astype(jnp.float32)[..., None] * inv_freq
    cos, sin = jnp.cos(ang), jnp.sin(ang)
    for i in range(DEPTH):
        x = hybrid_layer(x, p[i], cos, sin, g_pre_mix[i], w_in[i], g_cq[i], g_ckv[i],
                         w_q_up[i], w_kv_up[i], w_sb_o[i], w_mla_o[i], w_out[i],
                         g_post_mix[i], g_pre_mlp[i], w_up[i], w_down[i], g_post_mlp[i],
                         w_ple[i], g_ple[i], w_ple_gate[i])
    return x
```

```python
import functools

import jax
import jax.numpy as jnp
from jax import lax
from jax.experimental import pallas as pl
from jax.experimental.pallas import tpu as pltpu

F32 = jnp.float32
BF16 = jnp.bfloat16

D_MODEL = 2048
PLE_DIM = 256
SB_HEADS = 8
SB_HEAD_DIM = 128
MLA_HEADS = 8
MLA_NOPE_DIM = 128
MLA_ROPE_DIM = 64
MLA_V_DIM = 128
MLA_Q_RANK = 512
MLA_KV_RANK = 512
D_FF = 4 * D_MODEL
ROPE_THETA = 10000.0
EPS = 1e-6
SB_WIDTH = SB_HEADS * SB_HEAD_DIM
MLA_WIDTH = MLA_HEADS * MLA_V_DIM
MLA_QK_DIM = MLA_NOPE_DIM + MLA_ROPE_DIM
MLA_QK_PAD = 256

LANES = 128
MIB = 1024 * 1024

BLK_SB_Q = 0
BLK_SB_K = 8
BLK_SB_V = 16
BLK_CQ = 24
BLK_CKV = 28
BLK_GATE_SB = 32
BLK_GATE_MLA = 48
BLK_KROPE = 64
IN_BLOCKS = 65
IN_TILE_BLOCKS = 13

SB_LOG_UNDERFLOW = -104.0


def _params(semantics, vmem_mib):
    return pltpu.CompilerParams(dimension_semantics=semantics, vmem_limit_bytes=vmem_mib * MIB)


def _rms(xf, g):
    y = xf * lax.rsqrt(jnp.mean(xf * xf, axis=-1, keepdims=True) + EPS)
    return y * g


def _sigmoid(x):
    return 1.0 / (1.0 + jnp.exp(-x))


def _cat_blocks(ref, n):
    return jnp.concatenate([ref[c] for c in range(n)], axis=1)


def _in_proj_kernel(x_ref, g_ref, w_ref, o_ref, h_ref, *, nblk):
    @pl.when(pl.program_id(1) == 0)
    def _():
        h_ref[...] = _rms(x_ref[...], g_ref[...]).astype(BF16)

    acc = jnp.dot(h_ref[...], w_ref[...], preferred_element_type=F32)
    for c in range(nblk):
        o_ref[c] = acc[:, c * LANES:(c + 1) * LANES].astype(BF16)


def _in_proj(x2d, g, w):
    t, d = x2d.shape
    n = w.shape[1]
    tm = min(512, t)
    tn = IN_TILE_BLOCKS * LANES
    return pl.pallas_call(
        functools.partial(_in_proj_kernel, nblk=IN_TILE_BLOCKS),
        grid=(t // tm, n // tn),
        in_specs=[
            pl.BlockSpec((tm, d), lambda i, j: (i, 0)),
            pl.BlockSpec((1, d), lambda i, j: (0, 0)),
            pl.BlockSpec((d, tn), lambda i, j: (0, j)),
        ],
        out_specs=pl.BlockSpec((IN_TILE_BLOCKS, tm, LANES), lambda i, j: (j, i, 0)),
        out_shape=jax.ShapeDtypeStruct((n // LANES, t, LANES), BF16),
        scratch_shapes=[pltpu.VMEM((tm, d), BF16)],
        compiler_params=_params(("parallel", "arbitrary"), 48),
        name="in_proj",
    )(x2d, g, w)


def _sb_kernel(q_ref, k_ref, v_ref, u_ref, o_ref, acc_ref, carry_ref, *, blk):
    qi = pl.program_id(2)
    q = q_ref[0]
    acc_ref[...] = jnp.zeros_like(acc_ref)
    carry_ref[...] = jnp.zeros_like(carry_ref)
    row = lax.broadcasted_iota(jnp.int32, (blk, blk), 0)
    col = lax.broadcasted_iota(jnp.int32, (blk, blk), 1)
    scale = SB_HEAD_DIM ** -0.5

    def body(state):
        kb, _ = state
        start = pl.multiple_of(kb * blk, blk)
        k = k_ref[0, pl.ds(start, blk), :]
        v = v_ref[0, pl.ds(start, blk), :]
        z = lax.dot_general(q, k, (((1,), (1,)), ((), ())), preferred_element_type=F32) * scale
        softplus = jnp.maximum(z, 0.0) + jnp.log(1.0 + jnp.exp(-jnp.abs(z)))
        mask = col + (kb - qi) * blk < row
        log_fail = jnp.where(mask, -softplus, 0.0)
        hi = log_fail.astype(BF16)
        lo = (log_fail - hi.astype(F32)).astype(BF16)
        u = u_ref[...]
        sums = (jnp.dot(hi, u, preferred_element_type=F32)
                + jnp.dot(lo, u, preferred_element_type=F32))
        later = sums[:, :blk]
        total = sums[:, blk:]
        carry = carry_ref[...]
        carry_wide = jnp.concatenate([carry] * (blk // LANES), axis=1)
        w = jnp.where(mask, jnp.exp((z - softplus) + later + carry_wide), 0.0)
        acc_ref[...] += jnp.dot(w.astype(BF16), v, preferred_element_type=F32)
        carry = carry + total
        carry_ref[...] = carry
        return kb - 1, jnp.max(carry)

    def cond(state):
        kb, remaining = state
        return jnp.logical_and(kb >= 0, remaining > SB_LOG_UNDERFLOW)

    lax.while_loop(cond, body, (qi, jnp.float32(0.0)))
    o_ref[0] = acc_ref[...].astype(BF16)


def _sb_attn(proj, cum_mat, batch, seq):
    blk = min(256, seq)
    nq = seq // blk
    t = batch * seq
    return pl.pallas_call(
        functools.partial(_sb_kernel, blk=blk),
        grid=(batch, SB_HEADS, nq),
        in_specs=[
            pl.BlockSpec((1, blk, LANES), lambda b, h, i: (BLK_SB_Q + h, b * nq + i, 0)),
            pl.BlockSpec((1, seq, LANES), lambda b, h, i: (BLK_SB_K + h, b, 0)),
            pl.BlockSpec((1, seq, LANES), lambda b, h, i: (BLK_SB_V + h, b, 0)),
            pl.BlockSpec((blk, blk + LANES), lambda b, h, i: (0, 0)),
        ],
        out_specs=pl.BlockSpec((1, blk, LANES), lambda b, h, i: (h, b * nq + i, 0)),
        out_shape=jax.ShapeDtypeStruct((SB_HEADS, t, LANES), BF16),
        scratch_shapes=[pltpu.VMEM((blk, LANES), F32), pltpu.VMEM((blk, LANES), F32)],
        compiler_params=_params(("parallel", "parallel", "arbitrary"), 32),
        name="sb_attn",
    )(proj, proj, proj, cum_mat)


def _mla_prep_kernel(cq_ref, ckv_ref, kr_ref, pos_ref, invf_ref, gq_ref, gkv_ref, wq_ref, wkv_ref,
                     q_out, k_out, v_out):
    nlat = MLA_Q_RANK // LANES
    cq = _cat_blocks(cq_ref, nlat).astype(F32)
    ckv = _cat_blocks(ckv_ref, nlat).astype(F32)
    q = jnp.dot(_rms(cq, gq_ref[...]).astype(BF16), wq_ref[...], preferred_element_type=F32)
    kv = jnp.dot(_rms(ckv, gkv_ref[...]).astype(BF16), wkv_ref[...], preferred_element_type=F32)

    ang = pos_ref[...].astype(F32) * invf_ref[...]
    lane = lax.broadcasted_iota(jnp.int32, ang.shape, 1)
    half = MLA_ROPE_DIM // 2
    cos_t = jnp.where(lane < MLA_ROPE_DIM, jnp.cos(ang), 0.0)
    sin = jnp.sin(ang)
    sin_t = jnp.where(lane < half, -sin, jnp.where(lane < MLA_ROPE_DIM, sin, 0.0))

    def rope(a):
        return a * cos_t + pltpu.roll(a, half, axis=1) * sin_t

    k_rope = rope(kr_ref[0].astype(F32))
    scale = MLA_QK_DIM ** -0.5
    for h in range(MLA_HEADS):
        lo, hi = h * LANES, (h + 1) * LANES
        q_nope = q[:, lo:hi] * scale
        q_rope = rope(q[:, MLA_HEADS * LANES + lo:MLA_HEADS * LANES + hi]) * scale
        q_out[h] = jnp.concatenate([q_nope, q_rope], axis=1).astype(BF16)
        k_out[h] = jnp.concatenate([kv[:, lo:hi], k_rope], axis=1).astype(BF16)
        v_out[h] = kv[:, MLA_HEADS * LANES + lo:MLA_HEADS * LANES + hi].astype(BF16)


def _mla_prep(proj, pos, invf, g_cq, g_ckv, wq, wkv):
    t = proj.shape[1]
    tm = min(512, t)
    nlat = MLA_Q_RANK // LANES
    const = lambda i: (0, 0)
    return pl.pallas_call(
        _mla_prep_kernel,
        grid=(t // tm,),
        in_specs=[
            pl.BlockSpec((nlat, tm, LANES), lambda i: (BLK_CQ // nlat, i, 0)),
            pl.BlockSpec((nlat, tm, LANES), lambda i: (BLK_CKV // nlat, i, 0)),
            pl.BlockSpec((1, tm, LANES), lambda i: (BLK_KROPE, i, 0)),
            pl.BlockSpec((tm, 1), lambda i: (i, 0)),
            pl.BlockSpec((1, LANES), const),
            pl.BlockSpec((1, MLA_Q_RANK), const),
            pl.BlockSpec((1, MLA_KV_RANK), const),
            pl.BlockSpec(wq.shape, const),
            pl.BlockSpec(wkv.shape, const),
        ],
        out_specs=[
            pl.BlockSpec((MLA_HEADS, tm, MLA_QK_PAD), lambda i: (0, i, 0)),
            pl.BlockSpec((MLA_HEADS, tm, MLA_QK_PAD), lambda i: (0, i, 0)),
            pl.BlockSpec((MLA_HEADS, tm, LANES), lambda i: (0, i, 0)),
        ],
        out_shape=[
            jax.ShapeDtypeStruct((MLA_HEADS, t, MLA_QK_PAD), BF16),
            jax.ShapeDtypeStruct((MLA_HEADS, t, MLA_QK_PAD), BF16),
            jax.ShapeDtypeStruct((MLA_HEADS, t, LANES), BF16),
        ],
        compiler_params=_params(("parallel",), 48),
        name="mla_prep",
    )(proj, proj, proj, pos, invf, g_cq, g_ckv, wq, wkv)


def _mla_kernel(q_ref, k_ref, v_ref, o_ref, m_ref, l_ref, acc_ref, *, blk):
    qi = pl.program_id(2)
    q = q_ref[0]
    m_ref[...] = jnp.full_like(m_ref, -jnp.inf)
    l_ref[...] = jnp.zeros_like(l_ref)
    acc_ref[...] = jnp.zeros_like(acc_ref)

    def step(kb, diagonal):
        start = pl.multiple_of(kb * blk, blk)
        k = k_ref[0, pl.ds(start, blk), :]
        v = v_ref[0, pl.ds(start, blk), :]
        s = lax.dot_general(q, k, (((1,), (1,)), ((), ())), preferred_element_type=F32)
        if diagonal:
            row = lax.broadcasted_iota(jnp.int32, (blk, blk), 0)
            col = lax.broadcasted_iota(jnp.int32, (blk, blk), 1)
            s = jnp.where(col <= row, s, -jnp.inf)
        m_prev = m_ref[...]
        m_new = jnp.maximum(m_prev, jnp.max(s, axis=1, keepdims=True))
        alpha = jnp.exp(m_prev - m_new)
        p = jnp.exp(s - m_new)
        l_ref[...] = alpha * l_ref[...] + jnp.sum(p, axis=1, keepdims=True)
        acc_ref[...] = alpha * acc_ref[...] + jnp.dot(p.astype(BF16), v, preferred_element_type=F32)
        m_ref[...] = m_new

    def off_diagonal(kb, c):
        step(kb, False)
        return c

    lax.fori_loop(0, qi, off_diagonal, 0)
    step(qi, True)
    o_ref[0] = (acc_ref[...] / l_ref[...]).astype(BF16)


def _mla_attn(q, k, v, batch, seq):
    blk = min(512, seq)
    nq = seq // blk
    t = batch * seq
    return pl.pallas_call(
        functools.partial(_mla_kernel, blk=blk),
        grid=(batch, MLA_HEADS, nq),
        in_specs=[
            pl.BlockSpec((1, blk, MLA_QK_PAD), lambda b, h, i: (h, b * nq + i, 0)),
            pl.BlockSpec((1, seq, MLA_QK_PAD), lambda b, h, i: (h, b, 0)),
            pl.BlockSpec((1, seq, LANES), lambda b, h, i: (h, b, 0)),
        ],
        out_specs=pl.BlockSpec((1, blk, LANES), lambda b, h, i: (h, b * nq + i, 0)),
        out_shape=jax.ShapeDtypeStruct((MLA_HEADS, t, LANES), BF16),
        scratch_shapes=[pltpu.VMEM((blk, 1), F32), pltpu.VMEM((blk, 1), F32),
                        pltpu.VMEM((blk, LANES), F32)],
        compiler_params=_params(("parallel", "parallel", "arbitrary"), 48),
        name="mla_attn",
    )(q, k, v)


def _mix_kernel(x_ref, osb_ref, omla_ref, gsb_ref, gmla_ref, wsb_ref, wmla_ref, wout_ref,
                gpost_ref, gpre_ref, x1_ref, h2_ref):
    nd = D_MODEL // LANES
    a = jnp.dot(_cat_blocks(osb_ref, SB_HEADS), wsb_ref[...], preferred_element_type=F32)
    b = jnp.dot(_cat_blocks(omla_ref, MLA_HEADS), wmla_ref[...], preferred_element_type=F32)
    gate_sb = _cat_blocks(gsb_ref, nd).astype(F32)
    gate_mla = _cat_blocks(gmla_ref, nd).astype(F32)
    mixed = _sigmoid(gate_sb) * a + _sigmoid(gate_mla) * b
    y = jnp.dot(mixed.astype(BF16), wout_ref[...], preferred_element_type=F32)
    x1 = x_ref[...] + _rms(y, gpost_ref[...])
    x1_ref[...] = x1
    h2_ref[...] = _rms(x1, gpre_ref[...]).astype(BF16)


def _mix(x2d, o_sb, o_mla, proj, w_sb_o, w_mla_o, w_out, g_post, g_pre):
    t, d = x2d.shape
    tm = min(256, t)
    nd = d // LANES
    const = lambda i: (0, 0)
    single = pl.Buffered(1)
    return pl.pallas_call(
        _mix_kernel,
        grid=(t // tm,),
        in_specs=[
            pl.BlockSpec((tm, d), lambda i: (i, 0)),
            pl.BlockSpec((SB_HEADS, tm, LANES), lambda i: (0, i, 0)),
            pl.BlockSpec((MLA_HEADS, tm, LANES), lambda i: (0, i, 0)),
            pl.BlockSpec((nd, tm, LANES), lambda i: (BLK_GATE_SB // nd, i, 0)),
            pl.BlockSpec((nd, tm, LANES), lambda i: (BLK_GATE_MLA // nd, i, 0)),
            pl.BlockSpec(w_sb_o.shape, const, pipeline_mode=single),
            pl.BlockSpec(w_mla_o.shape, const, pipeline_mode=single),
            pl.BlockSpec(w_out.shape, const, pipeline_mode=single),
            pl.BlockSpec((1, d), const),
            pl.BlockSpec((1, d), const),
        ],
        out_specs=[
            pl.BlockSpec((tm, d), lambda i: (i, 0)),
            pl.BlockSpec((tm, d), lambda i: (i, 0)),
        ],
        out_shape=[
            jax.ShapeDtypeStruct((t, d), F32),
            jax.ShapeDtypeStruct((t, d), BF16),
        ],
        compiler_params=_params(("parallel",), 56),
        name="mix",
    )(x2d, o_sb, o_mla, proj, proj, w_sb_o, w_mla_o, w_out, g_post, g_pre)


def _mlp_kernel(h_ref, x1_ref, wup_ref, wdown_ref, g_ref, o_ref, acc_ref):
    f = pl.program_id(1)

    @pl.when(f == 0)
    def _():
        acc_ref[...] = jnp.zeros_like(acc_ref)

    u = jnp.dot(h_ref[...], wup_ref[...], preferred_element_type=F32)
    u = jnp.square(jnp.maximum(u, 0.0))
    acc_ref[...] += jnp.dot(u.astype(BF16), wdown_ref[...], preferred_element_type=F32)

    @pl.when(f == pl.num_programs(1) - 1)
    def _():
        o_ref[...] = x1_ref[...] + _rms(acc_ref[...], g_ref[...])


def _mlp(h2, x1, w_up, w_down, g_post):
    t, d = x1.shape
    dff = w_up.shape[1]
    tm = min(512, t)
    tf = 512
    return pl.pallas_call(
        _mlp_kernel,
        grid=(t // tm, dff // tf),
        in_specs=[
            pl.BlockSpec((tm, d), lambda i, f: (i, 0)),
            pl.BlockSpec((tm, d), lambda i, f: (i, 0)),
            pl.BlockSpec((d, tf), lambda i, f: (0, f)),
            pl.BlockSpec((tf, d), lambda i, f: (f, 0)),
            pl.BlockSpec((1, d), lambda i, f: (0, 0)),
        ],
        out_specs=pl.BlockSpec((tm, d), lambda i, f: (i, 0)),
        out_shape=jax.ShapeDtypeStruct((t, d), F32),
        scratch_shapes=[pltpu.VMEM((tm, d), F32)],
        compiler_params=_params(("parallel", "arbitrary"), 48),
        name="mlp",
    )(h2, x1, w_up, w_down, g_post)


def _ple_kernel(x_ref, p_ref, wple_ref, wgate_ref, g_ref, o_ref):
    x = x_ref[...]
    e = _rms(jnp.dot(p_ref[...].astype(BF16), wple_ref[...], preferred_element_type=F32), g_ref[...])
    gate = _sigmoid(jnp.dot(x.astype(BF16), wgate_ref[...], preferred_element_type=F32))
    o_ref[...] = x + gate * e


def _ple(x2, p2d, w_ple, w_gate, g_ple):
    t, d = x2.shape
    tm = min(512, t)
    const = lambda i: (0, 0)
    return pl.pallas_call(
        _ple_kernel,
        grid=(t // tm,),
        in_specs=[
            pl.BlockSpec((tm, d), lambda i: (i, 0)),
            pl.BlockSpec((tm, p2d.shape[1]), lambda i: (i, 0)),
            pl.BlockSpec(w_ple.shape, const),
            pl.BlockSpec(w_gate.shape, const),
            pl.BlockSpec((1, d), const),
        ],
        out_specs=pl.BlockSpec((tm, d), lambda i: (i, 0)),
        out_shape=jax.ShapeDtypeStruct((t, d), F32),
        compiler_params=_params(("parallel",), 48),
        name="ple",
    )(x2, p2d, w_ple, w_gate, g_ple)


def _reorder_w_in(w):
    o_kr = 3 * SB_WIDTH + MLA_Q_RANK + MLA_KV_RANK
    o_gate = o_kr + MLA_ROPE_DIM
    k_rope = w[:, o_kr:o_gate]
    return jnp.concatenate([w[:, :o_kr], w[:, o_gate:], k_rope, k_rope], axis=1).astype(BF16)


def _reorder_w_q_up(w):
    w3 = w.reshape(MLA_Q_RANK, MLA_HEADS, MLA_QK_DIM)
    nope = w3[:, :, :MLA_NOPE_DIM].reshape(MLA_Q_RANK, MLA_HEADS * MLA_NOPE_DIM)
    rope = w3[:, :, MLA_NOPE_DIM:]
    rope = jnp.concatenate([rope, rope], axis=-1).reshape(MLA_Q_RANK, MLA_HEADS * LANES)
    return jnp.concatenate([nope, rope], axis=1).astype(BF16)


def _reorder_w_kv_up(w):
    w3 = w.reshape(MLA_KV_RANK, MLA_HEADS, MLA_NOPE_DIM + MLA_V_DIM)
    k_nope = w3[:, :, :MLA_NOPE_DIM].reshape(MLA_KV_RANK, MLA_HEADS * MLA_NOPE_DIM)
    v = w3[:, :, MLA_NOPE_DIM:].reshape(MLA_KV_RANK, MLA_HEADS * MLA_V_DIM)
    return jnp.concatenate([k_nope, v], axis=1).astype(BF16)


def _cumsum_matrix(blk):
    j = jnp.arange(blk)[:, None]
    s = jnp.arange(blk)[None, :]
    return jnp.concatenate([(j > s).astype(BF16), jnp.ones((blk, LANES), BF16)], axis=1)


def _layer(x2d, p2d, pos, invf, batch, seq, g_pre_mix, w_in, g_cq, g_ckv, w_q_up, w_kv_up,
           w_sb_o, w_mla_o, w_out, g_post_mix, g_pre_mlp, w_up, w_down, g_post_mlp,
           w_ple, g_ple, w_ple_gate):
    row = lambda g: g.reshape(1, -1).astype(F32)
    proj = _in_proj(x2d, row(g_pre_mix), _reorder_w_in(w_in))
    o_sb = _sb_attn(proj, _cumsum_matrix(min(256, seq)), batch, seq)
    q, k, v = _mla_prep(proj, pos, invf, row(g_cq), row(g_ckv),
                        _reorder_w_q_up(w_q_up), _reorder_w_kv_up(w_kv_up))
    o_mla = _mla_attn(q, k, v, batch, seq)
    x1, h2 = _mix(x2d, o_sb, o_mla, proj, w_sb_o.astype(BF16), w_mla_o.astype(BF16),
                  w_out.astype(BF16), row(g_post_mix), row(g_pre_mlp))
    x2 = _mlp(h2, x1, w_up.astype(BF16), w_down.astype(BF16), row(g_post_mlp))
    return _ple(x2, p2d, w_ple.astype(BF16), w_ple_gate.astype(BF16), row(g_ple))


def kernel(x, p, positions, g_pre_mix, w_in, g_cq, g_ckv, w_q_up, w_kv_up, w_sb_o, w_mla_o, w_out, g_post_mix, g_pre_mlp, w_up, w_down, g_post_mlp, w_ple, g_ple, w_ple_gate):
    batch, seq, d = x.shape
    assert d == D_MODEL and seq % LANES == 0
    t = batch * seq
    half = MLA_ROPE_DIM // 2
    inv_freq = ROPE_THETA ** (-jnp.arange(half, dtype=F32) / half)
    invf = jnp.tile(inv_freq, LANES // half).reshape(1, LANES)
    pos = positions.reshape(t, 1).astype(jnp.int32)
    x2d = x.reshape(t, d)
    for i in range(p.shape[0]):
        x2d = _layer(x2d, p[i].reshape(t, PLE_DIM), pos, invf, batch, seq, g_pre_mix[i], w_in[i],
                     g_cq[i], g_ckv[i], w_q_up[i], w_kv_up[i], w_sb_o[i], w_mla_o[i], w_out[i],
                     g_post_mix[i], g_pre_mlp[i], w_up[i], w_down[i], g_post_mlp[i],
                     w_ple[i], g_ple[i], w_ple_gate[i])
    return x2d.reshape(batch, seq, d)
```

```python
import functools

import jax
import jax.numpy as jnp
from jax import lax
from jax.experimental import pallas as pl
from jax.experimental.pallas import tpu as pltpu

F32 = jnp.float32
BF16 = jnp.bfloat16

D_MODEL = 2048
PLE_DIM = 256
SB_HEADS = 8
SB_HEAD_DIM = 128
MLA_HEADS = 8
MLA_NOPE_DIM = 128
MLA_ROPE_DIM = 64
MLA_V_DIM = 128
MLA_Q_RANK = 512
MLA_KV_RANK = 512
D_FF = 4 * D_MODEL
ROPE_THETA = 10000.0
EPS = 1e-6
SB_WIDTH = SB_HEADS * SB_HEAD_DIM
MLA_WIDTH = MLA_HEADS * MLA_V_DIM
MLA_QK_DIM = MLA_NOPE_DIM + MLA_ROPE_DIM
MLA_QK_PAD = 256

LANES = 128
MIB = 1024 * 1024

BLK_SB_Q = 0
BLK_SB_K = 8
BLK_SB_V = 16
BLK_CQ = 24
BLK_CKV = 28
BLK_GATE_SB = 32
BLK_GATE_MLA = 48
BLK_KROPE = 64
IN_BLOCKS = 65
IN_TILE_BLOCKS = 13

LOG2E = 1.4426950408889634

SB_LOG2_UNDERFLOW = -104.0 * LOG2E


def _params(semantics, vmem_mib):
    return pltpu.CompilerParams(dimension_semantics=semantics, vmem_limit_bytes=vmem_mib * MIB)


def _rms(xf, g):
    y = xf * lax.rsqrt(jnp.mean(xf * xf, axis=-1, keepdims=True) + EPS)
    return y * g


def _sigmoid(x):
    return 1.0 / (1.0 + jnp.exp(-x))


def _cat_blocks(ref, n):
    return jnp.concatenate([ref[c] for c in range(n)], axis=1)


def _in_proj_kernel(x_ref, g_ref, w_ref, o_ref, h_ref, *, nblk):
    @pl.when(pl.program_id(1) == 0)
    def _():
        h_ref[...] = _rms(x_ref[...], g_ref[...]).astype(BF16)

    acc = jnp.dot(h_ref[...], w_ref[...], preferred_element_type=F32)
    for c in range(nblk):
        o_ref[c] = acc[:, c * LANES:(c + 1) * LANES].astype(BF16)


def _in_proj(x2d, g, w):
    t, d = x2d.shape
    n = w.shape[1]
    tm = min(1024, t)
    tn = IN_TILE_BLOCKS * LANES
    return pl.pallas_call(
        functools.partial(_in_proj_kernel, nblk=IN_TILE_BLOCKS),
        grid=(t // tm, n // tn),
        in_specs=[
            pl.BlockSpec((tm, d), lambda i, j: (i, 0)),
            pl.BlockSpec((1, d), lambda i, j: (0, 0)),
            pl.BlockSpec((d, tn), lambda i, j: (0, j)),
        ],
        out_specs=pl.BlockSpec((IN_TILE_BLOCKS, tm, LANES), lambda i, j: (j, i, 0)),
        out_shape=jax.ShapeDtypeStruct((n // LANES, t, LANES), BF16),
        scratch_shapes=[pltpu.VMEM((tm, d), BF16)],
        compiler_params=_params(("parallel", "arbitrary"), 56),
        name="in_proj",
    )(x2d, g, w)


def _sb_chunk(q, k, u, mask, blk):
    z = lax.dot_general(q, k, (((1,), (1,)), ((), ())), preferred_element_type=F32)
    softplus = jnp.maximum(z, 0.0) + jnp.log(1.0 + jnp.exp2(-jnp.abs(z))) * LOG2E
    log_fail = -softplus if mask is None else jnp.where(mask, -softplus, 0.0)
    hi = log_fail.astype(BF16)
    lo = (log_fail - hi.astype(F32)).astype(BF16)
    sums = jnp.dot(hi, u, preferred_element_type=F32) + jnp.dot(lo, u, preferred_element_type=F32)
    return (z - softplus) + sums[:, :blk], sums[:, blk:]


def _sb_kernel(q_ref, k_ref, v_ref, u_ref, o_ref, acc_ref, carry_ref, *, blk, heads):
    qi = pl.program_id(2)
    u = u_ref[...]

    def wide(c):
        return jnp.concatenate([c] * (blk // LANES), axis=1)

    def kv(g, kb):
        start = pl.multiple_of(kb * blk, blk)
        return k_ref[g, pl.ds(start, blk), :], v_ref[g, pl.ds(start, blk), :]

    def pv(w, v):
        return jnp.dot(w.astype(BF16), v, preferred_element_type=F32)

    row = lax.broadcasted_iota(jnp.int32, (blk, blk), 0)
    col = lax.broadcasted_iota(jnp.int32, (blk, blk), 1)
    mask = col < row
    no_prev = jnp.where(qi > 0, 0.0, -jnp.inf)
    remaining = []
    for g in range(heads):
        q = q_ref[g]
        k_a, v_a = kv(g, qi)
        k_b, v_b = kv(g, jnp.maximum(qi - 1, 0))
        logw_a, total_a = _sb_chunk(q, k_a, u, mask, blk)
        logw_b, total_b = _sb_chunk(q, k_b, u, None, blk)
        w_a = jnp.where(mask, jnp.exp2(logw_a), 0.0)
        w_b = jnp.exp2(logw_b + wide(total_a + no_prev))
        acc_ref[g] = pv(w_a, v_a) + pv(w_b, v_b)
        carry = total_a + total_b
        carry_ref[g] = carry
        remaining.append(jnp.max(carry))

    for g in range(heads):
        q = q_ref[g]

        def body(state, g=g, q=q):
            kb, _ = state
            k, v = kv(g, kb)
            logw, total = _sb_chunk(q, k, u, None, blk)
            carry = carry_ref[g]
            acc_ref[g] += pv(jnp.exp2(logw + wide(carry)), v)
            carry = carry + total
            carry_ref[g] = carry
            return kb - 1, jnp.max(carry)

        def cond(state):
            kb, left = state
            return jnp.logical_and(kb >= 0, left > SB_LOG2_UNDERFLOW)

        lax.while_loop(cond, body, (qi - 2, remaining[g]))
        o_ref[g] = acc_ref[g].astype(BF16)


def _sb_attn(proj, cum_mat, batch, seq):
    blk = min(256, seq)
    heads = 2
    nq = seq // blk
    t = batch * seq
    return pl.pallas_call(
        functools.partial(_sb_kernel, blk=blk, heads=heads),
        grid=(batch, SB_HEADS // heads, nq),
        in_specs=[
            pl.BlockSpec((heads, blk, LANES), lambda b, h, i: (BLK_SB_Q // heads + h, b * nq + i, 0)),
            pl.BlockSpec((heads, seq, LANES), lambda b, h, i: (BLK_SB_K // heads + h, b, 0)),
            pl.BlockSpec((heads, seq, LANES), lambda b, h, i: (BLK_SB_V // heads + h, b, 0)),
            pl.BlockSpec((blk, blk + LANES), lambda b, h, i: (0, 0)),
        ],
        out_specs=pl.BlockSpec((heads, blk, LANES), lambda b, h, i: (h, b * nq + i, 0)),
        out_shape=jax.ShapeDtypeStruct((SB_HEADS, t, LANES), BF16),
        scratch_shapes=[pltpu.VMEM((heads, blk, LANES), F32), pltpu.VMEM((heads, blk, LANES), F32)],
        compiler_params=_params(("parallel", "parallel", "arbitrary"), 32),
        name="sb_attn",
    )(proj, proj, proj, cum_mat)


def _mla_prep_kernel(cq_ref, ckv_ref, kr_ref, pos_ref, invf_ref, gq_ref, gkv_ref, wq_ref, wkv_ref,
                     q_out, k_out, v_out):
    nlat = MLA_Q_RANK // LANES
    cq = _cat_blocks(cq_ref, nlat).astype(F32)
    ckv = _cat_blocks(ckv_ref, nlat).astype(F32)
    q = jnp.dot(_rms(cq, gq_ref[...]).astype(BF16), wq_ref[...], preferred_element_type=F32)
    kv = jnp.dot(_rms(ckv, gkv_ref[...]).astype(BF16), wkv_ref[...], preferred_element_type=F32)

    ang = pos_ref[...].astype(F32) * invf_ref[...]
    lane = lax.broadcasted_iota(jnp.int32, ang.shape, 1)
    half = MLA_ROPE_DIM // 2
    cos_t = jnp.where(lane < MLA_ROPE_DIM, jnp.cos(ang), 0.0)
    sin = jnp.sin(ang)
    sin_t = jnp.where(lane < half, -sin, jnp.where(lane < MLA_ROPE_DIM, sin, 0.0))

    def rope(a):
        return a * cos_t + pltpu.roll(a, half, axis=1) * sin_t

    k_rope = rope(kr_ref[0].astype(F32))
    scale = MLA_QK_DIM ** -0.5 * LOG2E
    for h in range(MLA_HEADS):
        lo, hi = h * LANES, (h + 1) * LANES
        q_nope = q[:, lo:hi] * scale
        q_rope = rope(q[:, MLA_HEADS * LANES + lo:MLA_HEADS * LANES + hi]) * scale
        q_out[h] = jnp.concatenate([q_nope, q_rope], axis=1).astype(BF16)
        k_out[h] = jnp.concatenate([kv[:, lo:hi], k_rope], axis=1).astype(BF16)
        v_out[h] = kv[:, MLA_HEADS * LANES + lo:MLA_HEADS * LANES + hi].astype(BF16)


def _mla_prep(proj, pos, invf, g_cq, g_ckv, wq, wkv):
    t = proj.shape[1]
    tm = min(512, t)
    nlat = MLA_Q_RANK // LANES
    const = lambda i: (0, 0)
    return pl.pallas_call(
        _mla_prep_kernel,
        grid=(t // tm,),
        in_specs=[
            pl.BlockSpec((nlat, tm, LANES), lambda i: (BLK_CQ // nlat, i, 0)),
            pl.BlockSpec((nlat, tm, LANES), lambda i: (BLK_CKV // nlat, i, 0)),
            pl.BlockSpec((1, tm, LANES), lambda i: (BLK_KROPE, i, 0)),
            pl.BlockSpec((tm, 1), lambda i: (i, 0)),
            pl.BlockSpec((1, LANES), const),
            pl.BlockSpec((1, MLA_Q_RANK), const),
            pl.BlockSpec((1, MLA_KV_RANK), const),
            pl.BlockSpec(wq.shape, const),
            pl.BlockSpec(wkv.shape, const),
        ],
        out_specs=[
            pl.BlockSpec((MLA_HEADS, tm, MLA_QK_PAD), lambda i: (0, i, 0)),
            pl.BlockSpec((MLA_HEADS, tm, MLA_QK_PAD), lambda i: (0, i, 0)),
            pl.BlockSpec((MLA_HEADS, tm, LANES), lambda i: (0, i, 0)),
        ],
        out_shape=[
            jax.ShapeDtypeStruct((MLA_HEADS, t, MLA_QK_PAD), BF16),
            jax.ShapeDtypeStruct((MLA_HEADS, t, MLA_QK_PAD), BF16),
            jax.ShapeDtypeStruct((MLA_HEADS, t, LANES), BF16),
        ],
        compiler_params=_params(("parallel",), 48),
        name="mla_prep",
    )(proj, proj, proj, pos, invf, g_cq, g_ckv, wq, wkv)


def _mla_kernel(q_ref, k_ref, v_ref, o_ref, m_ref, acc_ref, *, blk, heads):
    qi = pl.program_id(2)
    m_ref[...] = jnp.full_like(m_ref, -jnp.inf)
    acc_ref[...] = jnp.zeros_like(acc_ref)

    def step(kb, diagonal):
        start = pl.multiple_of(kb * blk, blk)
        for g in range(heads):
            k = k_ref[g, pl.ds(start, blk), :]
            v = v_ref[g, pl.ds(start, blk), :]
            v_ext = jnp.concatenate([v, jnp.ones_like(v)], axis=1)
            s = lax.dot_general(q_ref[g], k, (((1,), (1,)), ((), ())),
                                preferred_element_type=F32)
            if diagonal:
                row = lax.broadcasted_iota(jnp.int32, (blk, blk), 0)
                col = lax.broadcasted_iota(jnp.int32, (blk, blk), 1)
                s = jnp.where(col <= row, s, -jnp.inf)
            m_prev = m_ref[g]
            m_new = jnp.maximum(m_prev, jnp.max(s, axis=1, keepdims=True))
            alpha = jnp.exp2(m_prev - m_new)
            p = jnp.exp2(s - jnp.concatenate([m_new] * (blk // LANES), axis=1))
            acc_ref[g] = (jnp.concatenate([alpha, alpha], axis=1) * acc_ref[g]
                          + jnp.dot(p.astype(BF16), v_ext, preferred_element_type=F32))
            m_ref[g] = m_new

    def off_diagonal(kb, c):
        step(kb, False)
        return c

    lax.fori_loop(0, qi, off_diagonal, 0)
    step(qi, True)
    for g in range(heads):
        acc = acc_ref[g]
        o_ref[g] = (acc[:, :LANES] / acc[:, LANES:]).astype(BF16)


def _mla_attn(q, k, v, batch, seq):
    blk = min(512, seq)
    heads = 4
    nq = seq // blk
    t = batch * seq
    resident = pl.Buffered(1)
    return pl.pallas_call(
        functools.partial(_mla_kernel, blk=blk, heads=heads),
        grid=(batch, MLA_HEADS // heads, nq),
        in_specs=[
            pl.BlockSpec((heads, blk, MLA_QK_PAD), lambda b, h, i: (h, b * nq + i, 0)),
            pl.BlockSpec((heads, seq, MLA_QK_PAD), lambda b, h, i: (h, b, 0), pipeline_mode=resident),
            pl.BlockSpec((heads, seq, LANES), lambda b, h, i: (h, b, 0), pipeline_mode=resident),
        ],
        out_specs=pl.BlockSpec((heads, blk, LANES), lambda b, h, i: (h, b * nq + i, 0)),
        out_shape=jax.ShapeDtypeStruct((MLA_HEADS, t, LANES), BF16),
        scratch_shapes=[pltpu.VMEM((heads, blk, LANES), F32),
                        pltpu.VMEM((heads, blk, 2 * LANES), F32)],
        compiler_params=_params(("parallel", "parallel", "arbitrary"), 48),
        name="mla_attn",
    )(q, k, v)


def _mix_kernel(x_ref, osb_ref, omla_ref, gsb_ref, gmla_ref, wsb_ref, wmla_ref, wout_ref,
                gpost_ref, gpre_ref, x1_ref, h2_ref):
    nd = D_MODEL // LANES
    a = jnp.dot(_cat_blocks(osb_ref, SB_HEADS), wsb_ref[...], preferred_element_type=F32)
    b = jnp.dot(_cat_blocks(omla_ref, MLA_HEADS), wmla_ref[...], preferred_element_type=F32)
    gate_sb = _cat_blocks(gsb_ref, nd).astype(F32)
    gate_mla = _cat_blocks(gmla_ref, nd).astype(F32)
    mixed = _sigmoid(gate_sb) * a + _sigmoid(gate_mla) * b
    y = jnp.dot(mixed.astype(BF16), wout_ref[...], preferred_element_type=F32)
    x1 = x_ref[...] + _rms(y, gpost_ref[...])
    x1_ref[...] = x1
    h2_ref[...] = _rms(x1, gpre_ref[...]).astype(BF16)


def _mix(x2d, o_sb, o_mla, proj, w_sb_o, w_mla_o, w_out, g_post, g_pre):
    t, d = x2d.shape
    tm = min(256, t)
    nd = d // LANES
    const = lambda i: (0, 0)
    single = pl.Buffered(1)
    return pl.pallas_call(
        _mix_kernel,
        grid=(t // tm,),
        in_specs=[
            pl.BlockSpec((tm, d), lambda i: (i, 0)),
            pl.BlockSpec((SB_HEADS, tm, LANES), lambda i: (0, i, 0)),
            pl.BlockSpec((MLA_HEADS, tm, LANES), lambda i: (0, i, 0)),
            pl.BlockSpec((nd, tm, LANES), lambda i: (BLK_GATE_SB // nd, i, 0)),
            pl.BlockSpec((nd, tm, LANES), lambda i: (BLK_GATE_MLA // nd, i, 0)),
            pl.BlockSpec(w_sb_o.shape, const, pipeline_mode=single),
            pl.BlockSpec(w_mla_o.shape, const, pipeline_mode=single),
            pl.BlockSpec(w_out.shape, const, pipeline_mode=single),
            pl.BlockSpec((1, d), const),
            pl.BlockSpec((1, d), const),
        ],
        out_specs=[
            pl.BlockSpec((tm, d), lambda i: (i, 0)),
            pl.BlockSpec((tm, d), lambda i: (i, 0)),
        ],
        out_shape=[
            jax.ShapeDtypeStruct((t, d), F32),
            jax.ShapeDtypeStruct((t, d), BF16),
        ],
        compiler_params=_params(("parallel",), 56),
        name="mix",
    )(x2d, o_sb, o_mla, proj, proj, w_sb_o, w_mla_o, w_out, g_post, g_pre)


def _mlp_kernel(h_ref, x1_ref, wup_ref, wdown_ref, g_ref, o_ref):
    f = pl.program_id(1)

    @pl.when(f == 0)
    def _():
        o_ref[...] = jnp.zeros_like(o_ref)

    u = jnp.dot(h_ref[...], wup_ref[...], preferred_element_type=F32)
    u = jnp.square(jnp.maximum(u, 0.0))
    o_ref[...] += jnp.dot(u.astype(BF16), wdown_ref[...], preferred_element_type=F32)

    @pl.when(f == pl.num_programs(1) - 1)
    def _():
        o_ref[...] = x1_ref[...] + _rms(o_ref[...], g_ref[...])


def _mlp(h2, x1, w_up, w_down, g_post):
    t, d = x1.shape
    dff = w_up.shape[1]
    tm = min(1024, t)
    tf = 512
    return pl.pallas_call(
        _mlp_kernel,
        grid=(t // tm, dff // tf),
        in_specs=[
            pl.BlockSpec((tm, d), lambda i, f: (i, 0)),
            pl.BlockSpec((tm, d), lambda i, f: (i, 0), pipeline_mode=pl.Buffered(1)),
            pl.BlockSpec((d, tf), lambda i, f: (0, f)),
            pl.BlockSpec((tf, d), lambda i, f: (f, 0)),
            pl.BlockSpec((1, d), lambda i, f: (0, 0)),
        ],
        out_specs=pl.BlockSpec((tm, d), lambda i, f: (i, 0)),
        out_shape=jax.ShapeDtypeStruct((t, d), F32),
        compiler_params=_params(("parallel", "arbitrary"), 58),
        name="mlp",
    )(h2, x1, w_up, w_down, g_post)


def _ple_kernel(x_ref, p_ref, wple_ref, wgate_ref, g_ref, o_ref):
    x = x_ref[...]
    e = _rms(jnp.dot(p_ref[...].astype(BF16), wple_ref[...], preferred_element_type=F32), g_ref[...])
    gate = _sigmoid(jnp.dot(x.astype(BF16), wgate_ref[...], preferred_element_type=F32))
    o_ref[...] = x + gate * e


def _ple(x2, p2d, w_ple, w_gate, g_ple):
    t, d = x2.shape
    tm = min(512, t)
    const = lambda i: (0, 0)
    return pl.pallas_call(
        _ple_kernel,
        grid=(t // tm,),
        in_specs=[
            pl.BlockSpec((tm, d), lambda i: (i, 0)),
            pl.BlockSpec((tm, p2d.shape[1]), lambda i: (i, 0)),
            pl.BlockSpec(w_ple.shape, const),
            pl.BlockSpec(w_gate.shape, const),
            pl.BlockSpec((1, d), const),
        ],
        out_specs=pl.BlockSpec((tm, d), lambda i: (i, 0)),
        out_shape=jax.ShapeDtypeStruct((t, d), F32),
        compiler_params=_params(("parallel",), 48),
        name="ple",
    )(x2, p2d, w_ple, w_gate, g_ple)


def _reorder_w_in(w):
    o_kr = 3 * SB_WIDTH + MLA_Q_RANK + MLA_KV_RANK
    o_gate = o_kr + MLA_ROPE_DIM
    k_rope = w[:, o_kr:o_gate]
    sb_q = w[:, :SB_WIDTH] * (SB_HEAD_DIM ** -0.5 * LOG2E)
    return jnp.concatenate([sb_q, w[:, SB_WIDTH:o_kr], w[:, o_gate:], k_rope, k_rope],
                           axis=1).astype(BF16)


def _reorder_w_q_up(w):
    w3 = w.reshape(MLA_Q_RANK, MLA_HEADS, MLA_QK_DIM)
    nope = w3[:, :, :MLA_NOPE_DIM].reshape(MLA_Q_RANK, MLA_HEADS * MLA_NOPE_DIM)
    rope = w3[:, :, MLA_NOPE_DIM:]
    rope = jnp.concatenate([rope, rope], axis=-1).reshape(MLA_Q_RANK, MLA_HEADS * LANES)
    return jnp.concatenate([nope, rope], axis=1).astype(BF16)


def _reorder_w_kv_up(w):
    w3 = w.reshape(MLA_KV_RANK, MLA_HEADS, MLA_NOPE_DIM + MLA_V_DIM)
    k_nope = w3[:, :, :MLA_NOPE_DIM].reshape(MLA_KV_RANK, MLA_HEADS * MLA_NOPE_DIM)
    v = w3[:, :, MLA_NOPE_DIM:].reshape(MLA_KV_RANK, MLA_HEADS * MLA_V_DIM)
    return jnp.concatenate([k_nope, v], axis=1).astype(BF16)


def _cumsum_matrix(blk):
    j = jnp.arange(blk)[:, None]
    s = jnp.arange(blk)[None, :]
    return jnp.concatenate([(j > s).astype(BF16), jnp.ones((blk, LANES), BF16)], axis=1)


def _layer(x2d, p2d, pos, invf, batch, seq, g_pre_mix, w_in, g_cq, g_ckv, w_q_up, w_kv_up,
           w_sb_o, w_mla_o, w_out, g_post_mix, g_pre_mlp, w_up, w_down, g_post_mlp,
           w_ple, g_ple, w_ple_gate):
    row = lambda g: g.reshape(1, -1).astype(F32)
    proj = _in_proj(x2d, row(g_pre_mix), _reorder_w_in(w_in))
    o_sb = _sb_attn(proj, _cumsum_matrix(min(256, seq)), batch, seq)
    q, k, v = _mla_prep(proj, pos, invf, row(g_cq), row(g_ckv),
                        _reorder_w_q_up(w_q_up), _reorder_w_kv_up(w_kv_up))
    o_mla = _mla_attn(q, k, v, batch, seq)
    x1, h2 = _mix(x2d, o_sb, o_mla, proj, w_sb_o.astype(BF16), w_mla_o.astype(BF16),
                  w_out.astype(BF16), row(g_post_mix), row(g_pre_mlp))
    x2 = _mlp(h2, x1, w_up.astype(BF16), w_down.astype(BF16), row(g_post_mlp))
    return _ple(x2, p2d, w_ple.astype(BF16), w_ple_gate.astype(BF16), row(g_ple))


def kernel(x, p, positions, g_pre_mix, w_in, g_cq, g_ckv, w_q_up, w_kv_up, w_sb_o, w_mla_o, w_out, g_post_mix, g_pre_mlp, w_up, w_down, g_post_mlp, w_ple, g_ple, w_ple_gate):
    batch, seq, d = x.shape
    assert d == D_MODEL and (seq % 512 == 0 or seq in (128, 256)), "sequence must tile into attention blocks"
    t = batch * seq
    half = MLA_ROPE_DIM // 2
    inv_freq = ROPE_THETA ** (-jnp.arange(half, dtype=F32) / half)
    invf = jnp.tile(inv_freq, LANES // half).reshape(1, LANES)
    pos = positions.reshape(t, 1).astype(jnp.int32)
    x2d = x.reshape(t, d)
    for i in range(p.shape[0]):
        x2d = _layer(x2d, p[i].reshape(t, PLE_DIM), pos, invf, batch, seq, g_pre_mix[i], w_in[i],
                     g_cq[i], g_ckv[i], w_q_up[i], w_kv_up[i], w_sb_o[i], w_mla_o[i], w_out[i],
                     g_post_mix[i], g_pre_mlp[i], w_up[i], w_down[i], g_post_mlp[i],
                     w_ple[i], g_ple[i], w_ple_gate[i])
    return x2d.reshape(batch, seq, d)
```

```python
import functools

import jax
import jax.numpy as jnp
from jax import lax
from jax.experimental import pallas as pl
from jax.experimental.pallas import tpu as pltpu

F32 = jnp.float32
BF16 = jnp.bfloat16

D_MODEL = 2048
PLE_DIM = 256
SB_HEADS = 8
SB_HEAD_DIM = 128
MLA_HEADS = 8
MLA_NOPE_DIM = 128
MLA_ROPE_DIM = 64
MLA_V_DIM = 128
MLA_Q_RANK = 512
MLA_KV_RANK = 512
D_FF = 4 * D_MODEL
ROPE_THETA = 10000.0
EPS = 1e-6
SB_WIDTH = SB_HEADS * SB_HEAD_DIM
MLA_WIDTH = MLA_HEADS * MLA_V_DIM
MLA_QK_DIM = MLA_NOPE_DIM + MLA_ROPE_DIM
MLA_QK_PAD = 256

LANES = 128
MIB = 1024 * 1024

BLK_SB_Q = 0
BLK_SB_K = 8
BLK_SB_V = 16
BLK_CQ = 24
BLK_CKV = 28
BLK_GATE_SB = 32
BLK_GATE_MLA = 48
IN_TILE_BLOCKS = 16

LOG2E = 1.4426950408889634

SB_LOG2_UNDERFLOW = -104.0 * LOG2E


def _params(semantics, vmem_mib):
    return pltpu.CompilerParams(dimension_semantics=semantics, vmem_limit_bytes=vmem_mib * MIB)


def _rms(xf, g):
    y = xf * lax.rsqrt(jnp.mean(xf * xf, axis=-1, keepdims=True) + EPS)
    return y * g


def _sigmoid(x):
    return 1.0 / (1.0 + jnp.exp(-x))


def _cat_blocks(ref, n):
    return jnp.concatenate([ref[c] for c in range(n)], axis=1)


def _in_proj_kernel(x_ref, g_ref, w_ref, wkr_ref, o_ref, kr_ref, h_ref, *, nblk):
    @pl.when(pl.program_id(1) == 0)
    def _():
        h = _rms(x_ref[...], g_ref[...]).astype(BF16)
        h_ref[...] = h
        kr_ref[...] = jnp.dot(h, wkr_ref[...], preferred_element_type=F32).astype(BF16)

    acc = jnp.dot(h_ref[...], w_ref[...], preferred_element_type=F32)
    for c in range(nblk):
        o_ref[c] = acc[:, c * LANES:(c + 1) * LANES].astype(BF16)


def _in_proj(x2d, g, w, w_kr):
    t, d = x2d.shape
    n = w.shape[1]
    tm = min(1024, t)
    tn = IN_TILE_BLOCKS * LANES
    return pl.pallas_call(
        functools.partial(_in_proj_kernel, nblk=IN_TILE_BLOCKS),
        grid=(t // tm, n // tn),
        in_specs=[
            pl.BlockSpec((tm, d), lambda i, j: (i, 0)),
            pl.BlockSpec((1, d), lambda i, j: (0, 0)),
            pl.BlockSpec((d, tn), lambda i, j: (0, j)),
            pl.BlockSpec((d, LANES), lambda i, j: (0, 0)),
        ],
        out_specs=[
            pl.BlockSpec((IN_TILE_BLOCKS, tm, LANES), lambda i, j: (j, i, 0)),
            pl.BlockSpec((tm, LANES), lambda i, j: (i, 0)),
        ],
        out_shape=[
            jax.ShapeDtypeStruct((n // LANES, t, LANES), BF16),
            jax.ShapeDtypeStruct((t, LANES), BF16),
        ],
        scratch_shapes=[pltpu.VMEM((tm, d), BF16)],
        compiler_params=_params(("parallel", "arbitrary"), 58),
        name="in_proj",
    )(x2d, g, w, w_kr)


def _sb_chunk(q, k, u, mask, blk):
    z = lax.dot_general(q, k, (((1,), (1,)), ((), ())), preferred_element_type=F32)
    softplus = jnp.maximum(z, 0.0) + jnp.log(1.0 + jnp.exp2(-jnp.abs(z))) * LOG2E
    log_fail = -softplus if mask is None else jnp.where(mask, -softplus, 0.0)
    hi = log_fail.astype(BF16)
    lo = (log_fail - hi.astype(F32)).astype(BF16)
    sums = jnp.dot(hi, u, preferred_element_type=F32) + jnp.dot(lo, u, preferred_element_type=F32)
    return (z - softplus) + sums[:, :blk], sums[:, blk:]


def _sb_kernel(q_ref, k_ref, v_ref, u_ref, o_ref, acc_ref, carry_ref, *, blk, heads):
    qi = pl.program_id(2)
    u = u_ref[...]

    def wide(c):
        return jnp.concatenate([c] * (blk // LANES), axis=1)

    def kv(g, kb):
        start = pl.multiple_of(kb * blk, blk)
        return k_ref[g, pl.ds(start, blk), :], v_ref[g, pl.ds(start, blk), :]

    def pv(w, v):
        return jnp.dot(w.astype(BF16), v, preferred_element_type=F32)

    row = lax.broadcasted_iota(jnp.int32, (blk, blk), 0)
    col = lax.broadcasted_iota(jnp.int32, (blk, blk), 1)
    mask = col < row
    no_prev = jnp.where(qi > 0, 0.0, -jnp.inf)
    remaining = []
    for g in range(heads):
        q = q_ref[g]
        k_a, v_a = kv(g, qi)
        k_b, v_b = kv(g, jnp.maximum(qi - 1, 0))
        logw_a, total_a = _sb_chunk(q, k_a, u, mask, blk)
        logw_b, total_b = _sb_chunk(q, k_b, u, None, blk)
        w_a = jnp.where(mask, jnp.exp2(logw_a), 0.0)
        w_b = jnp.exp2(logw_b + wide(total_a + no_prev))
        acc_ref[g] = pv(w_a, v_a) + pv(w_b, v_b)
        carry = total_a + total_b
        carry_ref[g] = carry
        remaining.append(jnp.max(carry))

    for g in range(heads):
        q = q_ref[g]

        def body(state, g=g, q=q):
            kb, _ = state
            k, v = kv(g, kb)
            logw, total = _sb_chunk(q, k, u, None, blk)
            carry = carry_ref[g]
            acc_ref[g] += pv(jnp.exp2(logw + wide(carry)), v)
            carry = carry + total
            carry_ref[g] = carry
            return kb - 1, jnp.max(carry)

        def cond(state):
            kb, left = state
            return jnp.logical_and(kb >= 0, left > SB_LOG2_UNDERFLOW)

        lax.while_loop(cond, body, (qi - 2, remaining[g]))
        o_ref[g] = acc_ref[g].astype(BF16)


def _sb_attn(proj, cum_mat, batch, seq):
    blk = min(256, seq)
    heads = 4
    nq = seq // blk
    t = batch * seq
    return pl.pallas_call(
        functools.partial(_sb_kernel, blk=blk, heads=heads),
        grid=(batch, SB_HEADS // heads, nq),
        in_specs=[
            pl.BlockSpec((heads, blk, LANES), lambda b, h, i: (BLK_SB_Q // heads + h, b * nq + i, 0)),
            pl.BlockSpec((heads, seq, LANES), lambda b, h, i: (BLK_SB_K // heads + h, b, 0),
                         pipeline_mode=pl.Buffered(1)),
            pl.BlockSpec((heads, seq, LANES), lambda b, h, i: (BLK_SB_V // heads + h, b, 0),
                         pipeline_mode=pl.Buffered(1)),
            pl.BlockSpec((blk, blk + LANES), lambda b, h, i: (0, 0)),
        ],
        out_specs=pl.BlockSpec((heads, blk, LANES), lambda b, h, i: (h, b * nq + i, 0)),
        out_shape=jax.ShapeDtypeStruct((SB_HEADS, t, LANES), BF16),
        scratch_shapes=[pltpu.VMEM((heads, blk, LANES), F32), pltpu.VMEM((heads, blk, LANES), F32)],
        compiler_params=_params(("parallel", "parallel", "arbitrary"), 32),
        name="sb_attn",
    )(proj, proj, proj, cum_mat)


def _mla_prep_kernel(cq_ref, ckv_ref, kr_ref, pos_ref, invf_ref, gq_ref, gkv_ref, wq_ref, wkv_ref,
                     q_out, k_out, v_out):
    nlat = MLA_Q_RANK // LANES
    cq = _cat_blocks(cq_ref, nlat).astype(F32)
    ckv = _cat_blocks(ckv_ref, nlat).astype(F32)
    q = jnp.dot(_rms(cq, gq_ref[...]).astype(BF16), wq_ref[...], preferred_element_type=F32)
    kv = jnp.dot(_rms(ckv, gkv_ref[...]).astype(BF16), wkv_ref[...], preferred_element_type=F32)

    ang = pos_ref[...].astype(F32) * invf_ref[...]
    lane = lax.broadcasted_iota(jnp.int32, ang.shape, 1)
    half = MLA_ROPE_DIM // 2
    cos_t = jnp.where(lane < MLA_ROPE_DIM, jnp.cos(ang), 0.0)
    sin = jnp.sin(ang)
    sin_t = jnp.where(lane < half, -sin, jnp.where(lane < MLA_ROPE_DIM, sin, 0.0))

    def rope(a):
        return a * cos_t + pltpu.roll(a, half, axis=1) * sin_t

    k_rope = rope(kr_ref[...].astype(F32))
    scale = MLA_QK_DIM ** -0.5 * LOG2E
    for h in range(MLA_HEADS):
        lo, hi = h * LANES, (h + 1) * LANES
        q_nope = q[:, lo:hi] * scale
        q_rope = rope(q[:, MLA_HEADS * LANES + lo:MLA_HEADS * LANES + hi]) * scale
        q_out[h] = jnp.concatenate([q_nope, q_rope], axis=1).astype(BF16)
        k_out[h] = jnp.concatenate([kv[:, lo:hi], k_rope], axis=1).astype(BF16)
        v_out[h] = kv[:, MLA_HEADS * LANES + lo:MLA_HEADS * LANES + hi].astype(BF16)


def _mla_prep(proj, kr, pos, invf, g_cq, g_ckv, wq, wkv):
    t = proj.shape[1]
    tm = min(512, t)
    nlat = MLA_Q_RANK // LANES
    const = lambda i: (0, 0)
    return pl.pallas_call(
        _mla_prep_kernel,
        grid=(t // tm,),
        in_specs=[
            pl.BlockSpec((nlat, tm, LANES), lambda i: (BLK_CQ // nlat, i, 0)),
            pl.BlockSpec((nlat, tm, LANES), lambda i: (BLK_CKV // nlat, i, 0)),
            pl.BlockSpec((tm, LANES), lambda i: (i, 0)),
            pl.BlockSpec((tm, 1), lambda i: (i, 0)),
            pl.BlockSpec((1, LANES), const),
            pl.BlockSpec((1, MLA_Q_RANK), const),
            pl.BlockSpec((1, MLA_KV_RANK), const),
            pl.BlockSpec(wq.shape, const),
            pl.BlockSpec(wkv.shape, const),
        ],
        out_specs=[
            pl.BlockSpec((MLA_HEADS, tm, MLA_QK_PAD), lambda i: (0, i, 0)),
            pl.BlockSpec((MLA_HEADS, tm, MLA_QK_PAD), lambda i: (0, i, 0)),
            pl.BlockSpec((MLA_HEADS, tm, LANES), lambda i: (0, i, 0)),
        ],
        out_shape=[
            jax.ShapeDtypeStruct((MLA_HEADS, t, MLA_QK_PAD), BF16),
            jax.ShapeDtypeStruct((MLA_HEADS, t, MLA_QK_PAD), BF16),
            jax.ShapeDtypeStruct((MLA_HEADS, t, LANES), BF16),
        ],
        compiler_params=_params(("parallel",), 48),
        name="mla_prep",
    )(proj, proj, kr, pos, invf, g_cq, g_ckv, wq, wkv)


def _mla_kernel(q_ref, k_ref, v_ref, o_ref, m_ref, acc_ref, *, blk, heads):
    qi = pl.program_id(2)
    m_ref[...] = jnp.full_like(m_ref, -jnp.inf)
    acc_ref[...] = jnp.zeros_like(acc_ref)

    def step(kb, diagonal):
        start = pl.multiple_of(kb * blk, blk)
        for g in range(heads):
            k = k_ref[g, pl.ds(start, blk), :]
            v = v_ref[g, pl.ds(start, blk), :]
            v_ext = jnp.concatenate([v, jnp.ones_like(v)], axis=1)
            s = lax.dot_general(q_ref[g], k, (((1,), (1,)), ((), ())),
                                preferred_element_type=F32)
            if diagonal:
                row = lax.broadcasted_iota(jnp.int32, (blk, blk), 0)
                col = lax.broadcasted_iota(jnp.int32, (blk, blk), 1)
                s = jnp.where(col <= row, s, -jnp.inf)
            m_prev = m_ref[g]
            m_new = jnp.maximum(m_prev, jnp.max(s, axis=1, keepdims=True))
            alpha = jnp.exp2(m_prev - m_new)
            p = jnp.exp2(s - jnp.concatenate([m_new] * (blk // LANES), axis=1))
            acc_ref[g] = (jnp.concatenate([alpha, alpha], axis=1) * acc_ref[g]
                          + jnp.dot(p.astype(BF16), v_ext, preferred_element_type=F32))
            m_ref[g] = m_new

    def off_diagonal_pair(j, c):
        step(2 * j, False)
        step(2 * j + 1, False)
        return c

    lax.fori_loop(0, qi // 2, off_diagonal_pair, 0)

    @pl.when(qi % 2 == 1)
    def _():
        step(qi - 1, False)

    step(qi, True)
    for g in range(heads):
        acc = acc_ref[g]
        o_ref[g] = (acc[:, :LANES] / acc[:, LANES:]).astype(BF16)


def _mla_attn(q, k, v, batch, seq):
    blk = min(512, seq)
    heads = 4
    nq = seq // blk
    t = batch * seq
    resident = pl.Buffered(1)
    return pl.pallas_call(
        functools.partial(_mla_kernel, blk=blk, heads=heads),
        grid=(batch, MLA_HEADS // heads, nq),
        in_specs=[
            pl.BlockSpec((heads, blk, MLA_QK_PAD), lambda b, h, i: (h, b * nq + i, 0)),
            pl.BlockSpec((heads, seq, MLA_QK_PAD), lambda b, h, i: (h, b, 0), pipeline_mode=resident),
            pl.BlockSpec((heads, seq, LANES), lambda b, h, i: (h, b, 0), pipeline_mode=resident),
        ],
        out_specs=pl.BlockSpec((heads, blk, LANES), lambda b, h, i: (h, b * nq + i, 0)),
        out_shape=jax.ShapeDtypeStruct((MLA_HEADS, t, LANES), BF16),
        scratch_shapes=[pltpu.VMEM((heads, blk, LANES), F32),
                        pltpu.VMEM((heads, blk, 2 * LANES), F32)],
        compiler_params=_params(("parallel", "parallel", "arbitrary"), 48),
        name="mla_attn",
    )(q, k, v)


def _mix_kernel(x_ref, osb_ref, omla_ref, gsb_ref, gmla_ref, wsb_ref, wmla_ref, wout_ref,
                gpost_ref, gpre_ref, x1_ref, h2_ref):
    nd = D_MODEL // LANES
    a = jnp.dot(_cat_blocks(osb_ref, SB_HEADS), wsb_ref[...], preferred_element_type=F32)
    b = jnp.dot(_cat_blocks(omla_ref, MLA_HEADS), wmla_ref[...], preferred_element_type=F32)
    gate_sb = _cat_blocks(gsb_ref, nd).astype(F32)
    gate_mla = _cat_blocks(gmla_ref, nd).astype(F32)
    mixed = _sigmoid(gate_sb) * a + _sigmoid(gate_mla) * b
    y = jnp.dot(mixed.astype(BF16), wout_ref[...], preferred_element_type=F32)
    x1 = x_ref[...] + _rms(y, gpost_ref[...])
    x1_ref[...] = x1
    h2_ref[...] = _rms(x1, gpre_ref[...]).astype(BF16)


def _mix(x2d, o_sb, o_mla, proj, w_sb_o, w_mla_o, w_out, g_post, g_pre):
    t, d = x2d.shape
    tm = min(256, t)
    nd = d // LANES
    const = lambda i: (0, 0)
    single = pl.Buffered(1)
    return pl.pallas_call(
        _mix_kernel,
        grid=(t // tm,),
        in_specs=[
            pl.BlockSpec((tm, d), lambda i: (i, 0)),
            pl.BlockSpec((SB_HEADS, tm, LANES), lambda i: (0, i, 0)),
            pl.BlockSpec((MLA_HEADS, tm, LANES), lambda i: (0, i, 0)),
            pl.BlockSpec((nd, tm, LANES), lambda i: (BLK_GATE_SB // nd, i, 0)),
            pl.BlockSpec((nd, tm, LANES), lambda i: (BLK_GATE_MLA // nd, i, 0)),
            pl.BlockSpec(w_sb_o.shape, const, pipeline_mode=single),
            pl.BlockSpec(w_mla_o.shape, const, pipeline_mode=single),
            pl.BlockSpec(w_out.shape, const, pipeline_mode=single),
            pl.BlockSpec((1, d), const),
            pl.BlockSpec((1, d), const),
        ],
        out_specs=[
            pl.BlockSpec((tm, d), lambda i: (i, 0)),
            pl.BlockSpec((tm, d), lambda i: (i, 0)),
        ],
        out_shape=[
            jax.ShapeDtypeStruct((t, d), F32),
            jax.ShapeDtypeStruct((t, d), BF16),
        ],
        compiler_params=_params(("parallel",), 56),
        name="mix",
    )(x2d, o_sb, o_mla, proj, proj, w_sb_o, w_mla_o, w_out, g_post, g_pre)


def _mlp_kernel(h_ref, x1_ref, wup_ref, wdown_ref, g_ref, o_ref):
    f = pl.program_id(1)

    @pl.when(f == 0)
    def _():
        o_ref[...] = jnp.zeros_like(o_ref)

    u = jnp.dot(h_ref[...], wup_ref[...], preferred_element_type=F32)
    u = jnp.square(jnp.maximum(u, 0.0))
    o_ref[...] += jnp.dot(u.astype(BF16), wdown_ref[...], preferred_element_type=F32)

    @pl.when(f == pl.num_programs(1) - 1)
    def _():
        o_ref[...] = x1_ref[...] + _rms(o_ref[...], g_ref[...])


def _mlp(h2, x1, w_up, w_down, g_post):
    t, d = x1.shape
    dff = w_up.shape[1]
    tm = min(1024, t)
    tf = 512
    return pl.pallas_call(
        _mlp_kernel,
        grid=(t // tm, dff // tf),
        in_specs=[
            pl.BlockSpec((tm, d), lambda i, f: (i, 0)),
            pl.BlockSpec((tm, d), lambda i, f: (i, 0), pipeline_mode=pl.Buffered(1)),
            pl.BlockSpec((d, tf), lambda i, f: (0, f)),
            pl.BlockSpec((tf, d), lambda i, f: (f, 0)),
            pl.BlockSpec((1, d), lambda i, f: (0, 0)),
        ],
        out_specs=pl.BlockSpec((tm, d), lambda i, f: (i, 0)),
        out_shape=jax.ShapeDtypeStruct((t, d), F32),
        compiler_params=_params(("parallel", "arbitrary"), 58),
        name="mlp",
    )(h2, x1, w_up, w_down, g_post)


def _ple_kernel(x_ref, p_ref, wple_ref, wgate_ref, g_ref, o_ref):
    x = x_ref[...]
    e = _rms(jnp.dot(p_ref[...].astype(BF16), wple_ref[...], preferred_element_type=F32), g_ref[...])
    gate = _sigmoid(jnp.dot(x.astype(BF16), wgate_ref[...], preferred_element_type=F32))
    o_ref[...] = x + gate * e


def _ple(x2, p2d, w_ple, w_gate, g_ple):
    t, d = x2.shape
    tm = min(512, t)
    const = lambda i: (0, 0)
    return pl.pallas_call(
        _ple_kernel,
        grid=(t // tm,),
        in_specs=[
            pl.BlockSpec((tm, d), lambda i: (i, 0)),
            pl.BlockSpec((tm, p2d.shape[1]), lambda i: (i, 0)),
            pl.BlockSpec(w_ple.shape, const),
            pl.BlockSpec(w_gate.shape, const),
            pl.BlockSpec((1, d), const),
        ],
        out_specs=pl.BlockSpec((tm, d), lambda i: (i, 0)),
        out_shape=jax.ShapeDtypeStruct((t, d), F32),
        compiler_params=_params(("parallel",), 48),
        name="ple",
    )(x2, p2d, w_ple, w_gate, g_ple)


def _reorder_w_in(w):
    o_kr = 3 * SB_WIDTH + MLA_Q_RANK + MLA_KV_RANK
    o_gate = o_kr + MLA_ROPE_DIM
    k_rope = w[:, o_kr:o_gate]
    sb_q = w[:, :SB_WIDTH] * (SB_HEAD_DIM ** -0.5 * LOG2E)
    main = jnp.concatenate([sb_q, w[:, SB_WIDTH:o_kr], w[:, o_gate:]], axis=1).astype(BF16)
    return main, jnp.concatenate([k_rope, k_rope], axis=1).astype(BF16)


def _reorder_w_q_up(w):
    w3 = w.reshape(MLA_Q_RANK, MLA_HEADS, MLA_QK_DIM)
    nope = w3[:, :, :MLA_NOPE_DIM].reshape(MLA_Q_RANK, MLA_HEADS * MLA_NOPE_DIM)
    rope = w3[:, :, MLA_NOPE_DIM:]
    rope = jnp.concatenate([rope, rope], axis=-1).reshape(MLA_Q_RANK, MLA_HEADS * LANES)
    return jnp.concatenate([nope, rope], axis=1).astype(BF16)


def _reorder_w_kv_up(w):
    w3 = w.reshape(MLA_KV_RANK, MLA_HEADS, MLA_NOPE_DIM + MLA_V_DIM)
    k_nope = w3[:, :, :MLA_NOPE_DIM].reshape(MLA_KV_RANK, MLA_HEADS * MLA_NOPE_DIM)
    v = w3[:, :, MLA_NOPE_DIM:].reshape(MLA_KV_RANK, MLA_HEADS * MLA_V_DIM)
    return jnp.concatenate([k_nope, v], axis=1).astype(BF16)


def _cumsum_matrix(blk):
    j = jnp.arange(blk)[:, None]
    s = jnp.arange(blk)[None, :]
    return jnp.concatenate([(j > s).astype(BF16), jnp.ones((blk, LANES), BF16)], axis=1)


def _layer(x2d, p2d, pos, invf, batch, seq, g_pre_mix, w_in, g_cq, g_ckv, w_q_up, w_kv_up,
           w_sb_o, w_mla_o, w_out, g_post_mix, g_pre_mlp, w_up, w_down, g_post_mlp,
           w_ple, g_ple, w_ple_gate):
    row = lambda g: g.reshape(1, -1).astype(F32)
    proj, kr = _in_proj(x2d, row(g_pre_mix), *_reorder_w_in(w_in))
    o_sb = _sb_attn(proj, _cumsum_matrix(min(256, seq)), batch, seq)
    q, k, v = _mla_prep(proj, kr, pos, invf, row(g_cq), row(g_ckv),
                        _reorder_w_q_up(w_q_up), _reorder_w_kv_up(w_kv_up))
    o_mla = _mla_attn(q, k, v, batch, seq)
    x1, h2 = _mix(x2d, o_sb, o_mla, proj, w_sb_o.astype(BF16), w_mla_o.astype(BF16),
                  w_out.astype(BF16), row(g_post_mix), row(g_pre_mlp))
    x2 = _mlp(h2, x1, w_up.astype(BF16), w_down.astype(BF16), row(g_post_mlp))
    return _ple(x2, p2d, w_ple.astype(BF16), w_ple_gate.astype(BF16), row(g_ple))


def kernel(x, p, positions, g_pre_mix, w_in, g_cq, g_ckv, w_q_up, w_kv_up, w_sb_o, w_mla_o, w_out, g_post_mix, g_pre_mlp, w_up, w_down, g_post_mlp, w_ple, g_ple, w_ple_gate):
    batch, seq, d = x.shape
    assert d == D_MODEL and (seq % 512 == 0 or seq in (128, 256)), "sequence must tile into attention blocks"
    t = batch * seq
    half = MLA_ROPE_DIM // 2
    inv_freq = ROPE_THETA ** (-jnp.arange(half, dtype=F32) / half)
    invf = jnp.tile(inv_freq, LANES // half).reshape(1, LANES)
    pos = positions.reshape(t, 1).astype(jnp.int32)
    x2d = x.reshape(t, d)
    for i in range(p.shape[0]):
        x2d = _layer(x2d, p[i].reshape(t, PLE_DIM), pos, invf, batch, seq, g_pre_mix[i], w_in[i],
                     g_cq[i], g_ckv[i], w_q_up[i], w_kv_up[i], w_sb_o[i], w_mla_o[i], w_out[i],
                     g_post_mix[i], g_pre_mlp[i], w_up[i], w_down[i], g_post_mlp[i],
                     w_ple[i], g_ple[i], w_ple_gate[i])
    return x2d.reshape(batch, seq, d)
```

```python
import functools

import jax
import jax.numpy as jnp
from jax import lax
from jax.experimental import pallas as pl
from jax.experimental.pallas import tpu as pltpu

F32 = jnp.float32
BF16 = jnp.bfloat16

D_MODEL = 2048
PLE_DIM = 256
SB_HEADS = 8
SB_HEAD_DIM = 128
MLA_HEADS = 8
MLA_NOPE_DIM = 128
MLA_ROPE_DIM = 64
MLA_V_DIM = 128
MLA_Q_RANK = 512
MLA_KV_RANK = 512
D_FF = 4 * D_MODEL
ROPE_THETA = 10000.0
EPS = 1e-6
SB_WIDTH = SB_HEADS * SB_HEAD_DIM
MLA_WIDTH = MLA_HEADS * MLA_V_DIM
MLA_QK_DIM = MLA_NOPE_DIM + MLA_ROPE_DIM
MLA_QK_PAD = 256

LANES = 128
MIB = 1024 * 1024

BLK_SB_Q = 0
BLK_SB_K = 8
BLK_SB_V = 16
BLK_CQ = 24
BLK_CKV = 28
BLK_GATE_SB = 32
BLK_GATE_MLA = 48
IN_TILE_BLOCKS = 16

LOG2E = 1.4426950408889634

SB_LOG2_UNDERFLOW = -104.0 * LOG2E


def _params(semantics, vmem_mib):
    return pltpu.CompilerParams(dimension_semantics=semantics, vmem_limit_bytes=vmem_mib * MIB)


def _rms(xf, g):
    y = xf * lax.rsqrt(jnp.mean(xf * xf, axis=-1, keepdims=True) + EPS)
    return y * g


def _sigmoid(x):
    return 1.0 / (1.0 + jnp.exp(-x))


def _cat_blocks(ref, n):
    return jnp.concatenate([ref[c] for c in range(n)], axis=1)


def _in_proj_kernel(x_ref, g_ref, w_ref, wkr_ref, o_ref, kr_ref, h_ref, *, nblk):
    @pl.when(pl.program_id(1) == 0)
    def _():
        h = _rms(x_ref[...], g_ref[...]).astype(BF16)
        h_ref[...] = h
        kr_ref[...] = jnp.dot(h, wkr_ref[...], preferred_element_type=F32).astype(BF16)

    acc = jnp.dot(h_ref[...], w_ref[...], preferred_element_type=F32)
    for c in range(nblk):
        o_ref[c] = acc[:, c * LANES:(c + 1) * LANES].astype(BF16)


def _in_proj(x2d, g, w, w_kr):
    t, d = x2d.shape
    n = w.shape[1]
    tm = min(1024, t)
    tn = IN_TILE_BLOCKS * LANES
    return pl.pallas_call(
        functools.partial(_in_proj_kernel, nblk=IN_TILE_BLOCKS),
        grid=(t // tm, n // tn),
        in_specs=[
            pl.BlockSpec((tm, d), lambda i, j: (i, 0)),
            pl.BlockSpec((1, d), lambda i, j: (0, 0)),
            pl.BlockSpec((d, tn), lambda i, j: (0, j)),
            pl.BlockSpec((d, LANES), lambda i, j: (0, 0)),
        ],
        out_specs=[
            pl.BlockSpec((IN_TILE_BLOCKS, tm, LANES), lambda i, j: (j, i, 0)),
            pl.BlockSpec((tm, LANES), lambda i, j: (i, 0)),
        ],
        out_shape=[
            jax.ShapeDtypeStruct((n // LANES, t, LANES), BF16),
            jax.ShapeDtypeStruct((t, LANES), BF16),
        ],
        scratch_shapes=[pltpu.VMEM((tm, d), BF16)],
        compiler_params=_params(("parallel", "arbitrary"), 58),
        name="in_proj",
    )(x2d, g, w, w_kr)


def _sb_chunk(q, k, u, mask, blk):
    z = lax.dot_general(q, k, (((1,), (1,)), ((), ())), preferred_element_type=F32)
    sign_bit = jnp.uint32(0x80000000)
    neg_abs = lax.bitcast_convert_type(lax.bitcast_convert_type(z, jnp.uint32) | sign_bit, F32)
    c = jnp.log(1.0 + jnp.exp2(neg_abs)) * LOG2E
    neg_part = jnp.minimum(z, 0.0)
    log_beta = neg_part - c
    log_fail = (neg_abs - neg_part) - c
    if mask is not None:
        log_fail = jnp.where(mask, log_fail, 0.0)
    hi = log_fail.astype(BF16)
    lo = (log_fail - hi.astype(F32)).astype(BF16)
    later = jnp.dot(hi, u, preferred_element_type=F32) + jnp.dot(lo, u, preferred_element_type=F32)
    return log_beta + later, jnp.sum(log_fail, axis=1, keepdims=True)


def _sb_kernel(q_ref, k_ref, v_ref, u_ref, o_ref, acc_ref, carry_ref, *, blk, heads):
    qi = pl.program_id(2)
    u = u_ref[...]

    def wide(c):
        return jnp.concatenate([c] * (blk // LANES), axis=1)

    def kv(g, kb):
        start = pl.multiple_of(kb * blk, blk)
        return k_ref[g, pl.ds(start, blk), :], v_ref[g, pl.ds(start, blk), :]

    def pv(w, v):
        return jnp.dot(w.astype(BF16), v, preferred_element_type=F32)

    row = lax.broadcasted_iota(jnp.int32, (blk, blk), 0)
    col = lax.broadcasted_iota(jnp.int32, (blk, blk), 1)
    mask = col < row
    no_prev = jnp.where(qi > 0, 0.0, -jnp.inf)
    remaining = []
    for g in range(heads):
        q = q_ref[g]
        k_a, v_a = kv(g, qi)
        k_b, v_b = kv(g, jnp.maximum(qi - 1, 0))
        logw_a, total_a = _sb_chunk(q, k_a, u, mask, blk)
        logw_b, total_b = _sb_chunk(q, k_b, u, None, blk)
        w_a = jnp.where(mask, jnp.exp2(logw_a), 0.0)
        w_b = jnp.exp2(logw_b + (total_a + no_prev))
        acc_ref[g] = pv(w_a, v_a) + pv(w_b, v_b)
        carry = total_a + total_b
        carry_ref[g] = jnp.broadcast_to(carry, (blk, LANES))
        remaining.append(jnp.max(carry))

    for g in range(heads):
        q = q_ref[g]

        def body(state, g=g, q=q):
            kb, _ = state
            k, v = kv(g, kb)
            logw, total = _sb_chunk(q, k, u, None, blk)
            carry = carry_ref[g]
            acc_ref[g] += pv(jnp.exp2(logw + wide(carry)), v)
            carry = carry + total
            carry_ref[g] = carry
            return kb - 1, jnp.max(carry)

        def cond(state):
            kb, left = state
            return jnp.logical_and(kb >= 0, left > SB_LOG2_UNDERFLOW)

        lax.while_loop(cond, body, (qi - 2, remaining[g]))
        o_ref[g] = acc_ref[g].astype(BF16)


def _sb_attn(proj, cum_mat, batch, seq):
    blk = min(256, seq)
    heads = 4
    nq = seq // blk
    t = batch * seq
    return pl.pallas_call(
        functools.partial(_sb_kernel, blk=blk, heads=heads),
        grid=(batch, SB_HEADS // heads, nq),
        in_specs=[
            pl.BlockSpec((heads, blk, LANES), lambda b, h, i: (BLK_SB_Q // heads + h, b * nq + i, 0)),
            pl.BlockSpec((heads, seq, LANES), lambda b, h, i: (BLK_SB_K // heads + h, b, 0),
                         pipeline_mode=pl.Buffered(1)),
            pl.BlockSpec((heads, seq, LANES), lambda b, h, i: (BLK_SB_V // heads + h, b, 0),
                         pipeline_mode=pl.Buffered(1)),
            pl.BlockSpec((blk, blk), lambda b, h, i: (0, 0)),
        ],
        out_specs=pl.BlockSpec((heads, blk, LANES), lambda b, h, i: (h, b * nq + i, 0)),
        out_shape=jax.ShapeDtypeStruct((SB_HEADS, t, LANES), BF16),
        scratch_shapes=[pltpu.VMEM((heads, blk, LANES), F32), pltpu.VMEM((heads, blk, LANES), F32)],
        compiler_params=_params(("parallel", "parallel", "arbitrary"), 32),
        name="sb_attn",
    )(proj, proj, proj, cum_mat)


def _mla_prep_kernel(cq_ref, ckv_ref, kr_ref, pos_ref, invf_ref, gq_ref, gkv_ref, wq_ref, wkv_ref,
                     q_out, k_out, v_out):
    nlat = MLA_Q_RANK // LANES
    cq = _cat_blocks(cq_ref, nlat).astype(F32)
    ckv = _cat_blocks(ckv_ref, nlat).astype(F32)
    q = jnp.dot(_rms(cq, gq_ref[...]).astype(BF16), wq_ref[...], preferred_element_type=F32)
    kv = jnp.dot(_rms(ckv, gkv_ref[...]).astype(BF16), wkv_ref[...], preferred_element_type=F32)

    half = MLA_ROPE_DIM // 2
    groups = LANES // half
    rows = pos_ref.shape[0] // groups
    lane = lax.broadcasted_iota(jnp.int32, (rows, LANES), 1)
    pos = pos_ref[...].astype(F32)
    packed = pos[(groups - 1) * rows:]
    for j in range(groups - 2, -1, -1):
        packed = jnp.where(lane < (j + 1) * half, pos[j * rows:(j + 1) * rows], packed)
    ang = packed * invf_ref[...]
    cos_p, sin_p = jnp.cos(ang), jnp.sin(ang)

    def unpack(t, j, k):
        shift = ((k - j) * half) % LANES
        return pltpu.roll(t, shift, axis=1) if shift else t

    cos_t, sin_t = [], []
    for j in range(groups):
        cos_t.append(jnp.where(lane < half, unpack(cos_p, j, 0),
                               jnp.where(lane < MLA_ROPE_DIM, unpack(cos_p, j, 1), 0.0)))
        sin_t.append(jnp.where(lane < half, -unpack(sin_p, j, 0),
                               jnp.where(lane < MLA_ROPE_DIM, unpack(sin_p, j, 1), 0.0)))
    cos_t = jnp.concatenate(cos_t, axis=0)
    sin_t = jnp.concatenate(sin_t, axis=0)

    def rope(a):
        return a * cos_t + pltpu.roll(a, half, axis=1) * sin_t

    k_rope = rope(kr_ref[...].astype(F32))
    scale = MLA_QK_DIM ** -0.5 * LOG2E
    for h in range(MLA_HEADS):
        lo, hi = h * LANES, (h + 1) * LANES
        q_nope = q[:, lo:hi] * scale
        q_rope = rope(q[:, MLA_HEADS * LANES + lo:MLA_HEADS * LANES + hi]) * scale
        q_out[h] = jnp.concatenate([q_nope, q_rope], axis=1).astype(BF16)
        k_out[h] = jnp.concatenate([kv[:, lo:hi], k_rope], axis=1).astype(BF16)
        v_out[h] = kv[:, MLA_HEADS * LANES + lo:MLA_HEADS * LANES + hi].astype(BF16)


def _mla_prep(proj, kr, pos, invf, g_cq, g_ckv, wq, wkv):
    t = proj.shape[1]
    tm = min(512, t)
    nlat = MLA_Q_RANK // LANES
    const = lambda i: (0, 0)
    return pl.pallas_call(
        _mla_prep_kernel,
        grid=(t // tm,),
        in_specs=[
            pl.BlockSpec((nlat, tm, LANES), lambda i: (BLK_CQ // nlat, i, 0)),
            pl.BlockSpec((nlat, tm, LANES), lambda i: (BLK_CKV // nlat, i, 0)),
            pl.BlockSpec((tm, LANES), lambda i: (i, 0)),
            pl.BlockSpec((tm, 1), lambda i: (i, 0)),
            pl.BlockSpec((1, LANES), const),
            pl.BlockSpec((1, MLA_Q_RANK), const),
            pl.BlockSpec((1, MLA_KV_RANK), const),
            pl.BlockSpec(wq.shape, const),
            pl.BlockSpec(wkv.shape, const),
        ],
        out_specs=[
            pl.BlockSpec((MLA_HEADS, tm, MLA_QK_PAD), lambda i: (0, i, 0)),
            pl.BlockSpec((MLA_HEADS, tm, MLA_QK_PAD), lambda i: (0, i, 0)),
            pl.BlockSpec((MLA_HEADS, tm, LANES), lambda i: (0, i, 0)),
        ],
        out_shape=[
            jax.ShapeDtypeStruct((MLA_HEADS, t, MLA_QK_PAD), BF16),
            jax.ShapeDtypeStruct((MLA_HEADS, t, MLA_QK_PAD), BF16),
            jax.ShapeDtypeStruct((MLA_HEADS, t, LANES), BF16),
        ],
        compiler_params=_params(("parallel",), 48),
        name="mla_prep",
    )(proj, proj, kr, pos, invf, g_cq, g_ckv, wq, wkv)


def _mla_kernel(q_ref, k_ref, v_ref, o_ref, m_ref, acc_ref, *, blk, heads, subs):
    qi = pl.program_id(2)
    m_ref[...] = jnp.full_like(m_ref, -jnp.inf)
    acc_ref[...] = jnp.zeros_like(acc_ref)

    def step(kb, kinds):
        start = pl.multiple_of(kb * blk, blk)
        for g in range(heads):
            k = k_ref[g, pl.ds(start, blk), :]
            v = v_ref[g, pl.ds(start, blk), :]
            v_ext = jnp.concatenate([v, jnp.ones_like(v)], axis=1)
            for r, kind in enumerate(kinds):
                if kind is None:
                    continue
                rows = slice(r * blk, (r + 1) * blk)
                s = lax.dot_general(q_ref[g, rows, :], k, (((1,), (1,)), ((), ())),
                                    preferred_element_type=F32)
                if kind == "diag":
                    row = lax.broadcasted_iota(jnp.int32, (blk, blk), 0)
                    col = lax.broadcasted_iota(jnp.int32, (blk, blk), 1)
                    s = jnp.where(col <= row, s, -jnp.inf)
                m_prev = m_ref[g, rows, :]
                m_new = jnp.maximum(m_prev, jnp.max(s, axis=1, keepdims=True))
                alpha = jnp.exp2(m_prev - m_new)
                p = jnp.exp2(s - jnp.concatenate([m_new] * (blk // LANES), axis=1))
                acc_ref[g, rows, :] = (jnp.concatenate([alpha, alpha], axis=1) * acc_ref[g, rows, :]
                                       + jnp.dot(p.astype(BF16), v_ext, preferred_element_type=F32))
                m_ref[g, rows, :] = m_new

    def below_diagonal(j, c):
        for d in range(subs):
            step(subs * j + d, ("full",) * subs)
        return c

    lax.fori_loop(0, qi, below_diagonal, 0)
    for d in range(subs):
        step(subs * qi + d, tuple("diag" if r == d else ("full" if r > d else None)
                                  for r in range(subs)))
    for g in range(heads):
        acc = acc_ref[g]
        o_ref[g] = (acc[:, :LANES] / acc[:, LANES:]).astype(BF16)


def _mla_attn(q, k, v, batch, seq):
    blk = min(512, seq)
    subs = min(2, seq // blk)
    heads = 4
    tq = subs * blk
    assert seq % tq == 0, "sequence must tile into MLA query tiles"
    nq = seq // tq
    t = batch * seq
    resident = pl.Buffered(1)
    return pl.pallas_call(
        functools.partial(_mla_kernel, blk=blk, heads=heads, subs=subs),
        grid=(batch, MLA_HEADS // heads, nq),
        in_specs=[
            pl.BlockSpec((heads, tq, MLA_QK_PAD), lambda b, h, i: (h, b * nq + i, 0)),
            pl.BlockSpec((heads, seq, MLA_QK_PAD), lambda b, h, i: (h, b, 0), pipeline_mode=resident),
            pl.BlockSpec((heads, seq, LANES), lambda b, h, i: (h, b, 0), pipeline_mode=resident),
        ],
        out_specs=pl.BlockSpec((heads, tq, LANES), lambda b, h, i: (h, b * nq + i, 0)),
        out_shape=jax.ShapeDtypeStruct((MLA_HEADS, t, LANES), BF16),
        scratch_shapes=[pltpu.VMEM((heads, tq, LANES), F32),
                        pltpu.VMEM((heads, tq, 2 * LANES), F32)],
        compiler_params=_params(("parallel", "parallel", "arbitrary"), 52),
        name="mla_attn",
    )(q, k, v)


def _mix_kernel(x_ref, osb_ref, omla_ref, gsb_ref, gmla_ref, wsb_ref, wmla_ref, wout_ref,
                gpost_ref, gpre_ref, x1_ref, h2_ref):
    nd = D_MODEL // LANES
    a = jnp.dot(_cat_blocks(osb_ref, SB_HEADS), wsb_ref[...], preferred_element_type=F32)
    b = jnp.dot(_cat_blocks(omla_ref, MLA_HEADS), wmla_ref[...], preferred_element_type=F32)
    gate_sb = _cat_blocks(gsb_ref, nd).astype(F32)
    gate_mla = _cat_blocks(gmla_ref, nd).astype(F32)
    mixed = _sigmoid(gate_sb) * a + _sigmoid(gate_mla) * b
    y = jnp.dot(mixed.astype(BF16), wout_ref[...], preferred_element_type=F32)
    x1 = x_ref[...] + _rms(y, gpost_ref[...])
    x1_ref[...] = x1
    h2_ref[...] = _rms(x1, gpre_ref[...]).astype(BF16)


def _mix(x2d, o_sb, o_mla, proj, w_sb_o, w_mla_o, w_out, g_post, g_pre):
    t, d = x2d.shape
    tm = min(256, t)
    nd = d // LANES
    const = lambda i: (0, 0)
    single = pl.Buffered(1)
    return pl.pallas_call(
        _mix_kernel,
        grid=(t // tm,),
        in_specs=[
            pl.BlockSpec((tm, d), lambda i: (i, 0)),
            pl.BlockSpec((SB_HEADS, tm, LANES), lambda i: (0, i, 0)),
            pl.BlockSpec((MLA_HEADS, tm, LANES), lambda i: (0, i, 0)),
            pl.BlockSpec((nd, tm, LANES), lambda i: (BLK_GATE_SB // nd, i, 0)),
            pl.BlockSpec((nd, tm, LANES), lambda i: (BLK_GATE_MLA // nd, i, 0)),
            pl.BlockSpec(w_sb_o.shape, const, pipeline_mode=single),
            pl.BlockSpec(w_mla_o.shape, const, pipeline_mode=single),
            pl.BlockSpec(w_out.shape, const, pipeline_mode=single),
            pl.BlockSpec((1, d), const),
            pl.BlockSpec((1, d), const),
        ],
        out_specs=[
            pl.BlockSpec((tm, d), lambda i: (i, 0)),
            pl.BlockSpec((tm, d), lambda i: (i, 0)),
        ],
        out_shape=[
            jax.ShapeDtypeStruct((t, d), F32),
            jax.ShapeDtypeStruct((t, d), BF16),
        ],
        compiler_params=_params(("parallel",), 56),
        name="mix",
    )(x2d, o_sb, o_mla, proj, proj, w_sb_o, w_mla_o, w_out, g_post, g_pre)


def _mlp_kernel(h_ref, x1_ref, wup_ref, wdown_ref, g_ref, o_ref):
    f = pl.program_id(1)

    @pl.when(f == 0)
    def _():
        o_ref[...] = jnp.zeros_like(o_ref)

    u = jnp.dot(h_ref[...], wup_ref[...], preferred_element_type=F32)
    u = jnp.square(jnp.maximum(u, 0.0))
    o_ref[...] += jnp.dot(u.astype(BF16), wdown_ref[...], preferred_element_type=F32)

    @pl.when(f == pl.num_programs(1) - 1)
    def _():
        o_ref[...] = x1_ref[...] + _rms(o_ref[...], g_ref[...])


def _mlp(h2, x1, w_up, w_down, g_post):
    t, d = x1.shape
    dff = w_up.shape[1]
    tm = min(1024, t)
    tf = 512
    return pl.pallas_call(
        _mlp_kernel,
        grid=(t // tm, dff // tf),
        in_specs=[
            pl.BlockSpec((tm, d), lambda i, f: (i, 0)),
            pl.BlockSpec((tm, d), lambda i, f: (i, 0), pipeline_mode=pl.Buffered(1)),
            pl.BlockSpec((d, tf), lambda i, f: (0, f)),
            pl.BlockSpec((tf, d), lambda i, f: (f, 0)),
            pl.BlockSpec((1, d), lambda i, f: (0, 0)),
        ],
        out_specs=pl.BlockSpec((tm, d), lambda i, f: (i, 0)),
        out_shape=jax.ShapeDtypeStruct((t, d), F32),
        compiler_params=_params(("parallel", "arbitrary"), 58),
        name="mlp",
    )(h2, x1, w_up, w_down, g_post)


def _ple_kernel(x_ref, p_ref, wple_ref, wgate_ref, g_ref, o_ref):
    x = x_ref[...]
    e = _rms(jnp.dot(p_ref[...].astype(BF16), wple_ref[...], preferred_element_type=F32), g_ref[...])
    gate = _sigmoid(jnp.dot(x.astype(BF16), wgate_ref[...], preferred_element_type=F32))
    o_ref[...] = x + gate * e


def _ple(x2, p2d, w_ple, w_gate, g_ple):
    t, d = x2.shape
    tm = min(512, t)
    const = lambda i: (0, 0)
    return pl.pallas_call(
        _ple_kernel,
        grid=(t // tm,),
        in_specs=[
            pl.BlockSpec((tm, d), lambda i: (i, 0)),
            pl.BlockSpec((tm, p2d.shape[1]), lambda i: (i, 0)),
            pl.BlockSpec(w_ple.shape, const),
            pl.BlockSpec(w_gate.shape, const),
            pl.BlockSpec((1, d), const),
        ],
        out_specs=pl.BlockSpec((tm, d), lambda i: (i, 0)),
        out_shape=jax.ShapeDtypeStruct((t, d), F32),
        compiler_params=_params(("parallel",), 48),
        name="ple",
    )(x2, p2d, w_ple, w_gate, g_ple)


def _reorder_w_in(w):
    o_kr = 3 * SB_WIDTH + MLA_Q_RANK + MLA_KV_RANK
    o_gate = o_kr + MLA_ROPE_DIM
    k_rope = w[:, o_kr:o_gate]
    sb_q = w[:, :SB_WIDTH] * (SB_HEAD_DIM ** -0.5 * LOG2E)
    main = jnp.concatenate([sb_q, w[:, SB_WIDTH:o_kr], w[:, o_gate:]], axis=1).astype(BF16)
    return main, jnp.concatenate([k_rope, k_rope], axis=1).astype(BF16)


def _reorder_w_q_up(w):
    w3 = w.reshape(MLA_Q_RANK, MLA_HEADS, MLA_QK_DIM)
    nope = w3[:, :, :MLA_NOPE_DIM].reshape(MLA_Q_RANK, MLA_HEADS * MLA_NOPE_DIM)
    rope = w3[:, :, MLA_NOPE_DIM:]
    rope = jnp.concatenate([rope, rope], axis=-1).reshape(MLA_Q_RANK, MLA_HEADS * LANES)
    return jnp.concatenate([nope, rope], axis=1).astype(BF16)


def _reorder_w_kv_up(w):
    w3 = w.reshape(MLA_KV_RANK, MLA_HEADS, MLA_NOPE_DIM + MLA_V_DIM)
    k_nope = w3[:, :, :MLA_NOPE_DIM].reshape(MLA_KV_RANK, MLA_HEADS * MLA_NOPE_DIM)
    v = w3[:, :, MLA_NOPE_DIM:].reshape(MLA_KV_RANK, MLA_HEADS * MLA_V_DIM)
    return jnp.concatenate([k_nope, v], axis=1).astype(BF16)


def _cumsum_matrix(blk):
    j = jnp.arange(blk)[:, None]
    s = jnp.arange(blk)[None, :]
    return (j > s).astype(BF16)


def _layer(x2d, p2d, pos, invf, batch, seq, g_pre_mix, w_in, g_cq, g_ckv, w_q_up, w_kv_up,
           w_sb_o, w_mla_o, w_out, g_post_mix, g_pre_mlp, w_up, w_down, g_post_mlp,
           w_ple, g_ple, w_ple_gate):
    row = lambda g: g.reshape(1, -1).astype(F32)
    proj, kr = _in_proj(x2d, row(g_pre_mix), *_reorder_w_in(w_in))
    o_sb = _sb_attn(proj, _cumsum_matrix(min(256, seq)), batch, seq)
    q, k, v = _mla_prep(proj, kr, pos, invf, row(g_cq), row(g_ckv),
                        _reorder_w_q_up(w_q_up), _reorder_w_kv_up(w_kv_up))
    o_mla = _mla_attn(q, k, v, batch, seq)
    x1, h2 = _mix(x2d, o_sb, o_mla, proj, w_sb_o.astype(BF16), w_mla_o.astype(BF16),
                  w_out.astype(BF16), row(g_post_mix), row(g_pre_mlp))
    x2 = _mlp(h2, x1, w_up.astype(BF16), w_down.astype(BF16), row(g_post_mlp))
    return _ple(x2, p2d, w_ple.astype(BF16), w_ple_gate.astype(BF16), row(g_ple))


def kernel(x, p, positions, g_pre_mix, w_in, g_cq, g_ckv, w_q_up, w_kv_up, w_sb_o, w_mla_o, w_out, g_post_mix, g_pre_mlp, w_up, w_down, g_post_mlp, w_ple, g_ple, w_ple_gate):
    batch, seq, d = x.shape
    assert d == D_MODEL and seq % 256 == 0, "sequence must tile into stick-breaking blocks"
    t = batch * seq
    half = MLA_ROPE_DIM // 2
    inv_freq = ROPE_THETA ** (-jnp.arange(half, dtype=F32) / half)
    invf = jnp.tile(inv_freq, LANES // half).reshape(1, LANES)
    pos = positions.reshape(t, 1).astype(jnp.int32)
    x2d = x.reshape(t, d)
    for i in range(p.shape[0]):
        x2d = _layer(x2d, p[i].reshape(t, PLE_DIM), pos, invf, batch, seq, g_pre_mix[i], w_in[i],
                     g_cq[i], g_ckv[i], w_q_up[i], w_kv_up[i], w_sb_o[i], w_mla_o[i], w_out[i],
                     g_post_mix[i], g_pre_mlp[i], w_up[i], w_down[i], g_post_mlp[i],
                     w_ple[i], g_ple[i], w_ple_gate[i])
    return x2d.reshape(batch, seq, d)
```

```python
import functools

import jax
import jax.numpy as jnp
from jax import lax
from jax.experimental import pallas as pl
from jax.experimental.pallas import tpu as pltpu

F32 = jnp.float32
BF16 = jnp.bfloat16

D_MODEL = 2048
PLE_DIM = 256
SB_HEADS = 8
SB_HEAD_DIM = 128
MLA_HEADS = 8
MLA_NOPE_DIM = 128
MLA_ROPE_DIM = 64
MLA_V_DIM = 128
MLA_Q_RANK = 512
MLA_KV_RANK = 512
D_FF = 4 * D_MODEL
ROPE_THETA = 10000.0
EPS = 1e-6
SB_WIDTH = SB_HEADS * SB_HEAD_DIM
MLA_WIDTH = MLA_HEADS * MLA_V_DIM
MLA_QK_DIM = MLA_NOPE_DIM + MLA_ROPE_DIM
MLA_QK_PAD = 256

LANES = 128
MIB = 1024 * 1024

BLK_SB_Q = 0
BLK_SB_K = 8
BLK_SB_V = 16
BLK_CQ = 24
BLK_CKV = 28
BLK_GATE_SB = 32
BLK_GATE_MLA = 48
IN_TILE_BLOCKS = 16
MLP_TF = 1024

LOG2E = 1.4426950408889634

SB_LOG2_UNDERFLOW = -104.0 * LOG2E


def _params(semantics, vmem_mib):
    return pltpu.CompilerParams(dimension_semantics=semantics, vmem_limit_bytes=vmem_mib * MIB)


def _rms(xf, g):
    y = xf * lax.rsqrt(jnp.mean(xf * xf, axis=-1, keepdims=True) + EPS)
    return y * g


def _sigmoid(x):
    return 1.0 / (1.0 + jnp.exp(-x))


def _cat_blocks(ref, n):
    return jnp.concatenate([ref[c] for c in range(n)], axis=1)


def _in_proj_kernel(x_ref, g_ref, w_ref, wkr_ref, o_ref, kr_ref, h_ref, *, nblk):
    @pl.when(pl.program_id(1) == 0)
    def _():
        h = _rms(x_ref[...], g_ref[...]).astype(BF16)
        h_ref[...] = h
        kr_ref[...] = jnp.dot(h, wkr_ref[...], preferred_element_type=F32).astype(BF16)

    acc = jnp.dot(h_ref[...], w_ref[0], preferred_element_type=F32)
    for c in range(nblk):
        o_ref[c] = acc[:, c * LANES:(c + 1) * LANES].astype(BF16)


def _in_proj(x2d, g, w, w_kr):
    t, d = x2d.shape
    ntiles, _, tn = w.shape
    n = ntiles * tn
    tm = min(1024, t)
    return pl.pallas_call(
        functools.partial(_in_proj_kernel, nblk=IN_TILE_BLOCKS),
        grid=(t // tm, n // tn),
        in_specs=[
            pl.BlockSpec((tm, d), lambda i, j: (i, 0)),
            pl.BlockSpec((1, d), lambda i, j: (0, 0)),
            pl.BlockSpec((1, d, tn), lambda i, j: (j, 0, 0)),
            pl.BlockSpec((d, LANES), lambda i, j: (0, 0)),
        ],
        out_specs=[
            pl.BlockSpec((IN_TILE_BLOCKS, tm, LANES), lambda i, j: (j, i, 0)),
            pl.BlockSpec((tm, LANES), lambda i, j: (i, 0)),
        ],
        out_shape=[
            jax.ShapeDtypeStruct((n // LANES, t, LANES), BF16),
            jax.ShapeDtypeStruct((t, LANES), BF16),
        ],
        scratch_shapes=[pltpu.VMEM((tm, d), BF16)],
        compiler_params=_params(("parallel", "arbitrary"), 58),
        name="in_proj",
    )(x2d, g, w, w_kr)


def _sb_chunk(q, k, u, mask, blk):
    z = lax.dot_general(q, k, (((1,), (1,)), ((), ())), preferred_element_type=F32)
    sign_bit = jnp.uint32(0x80000000)
    neg_abs = lax.bitcast_convert_type(lax.bitcast_convert_type(z, jnp.uint32) | sign_bit, F32)
    c = jnp.log(1.0 + jnp.exp2(neg_abs)) * LOG2E
    neg_part = jnp.minimum(z, 0.0)
    log_beta = neg_part - c
    log_fail = (neg_abs - neg_part) - c
    if mask is not None:
        log_fail = jnp.where(mask, log_fail, 0.0)
    hi = log_fail.astype(BF16)
    lo = (log_fail - hi.astype(F32)).astype(BF16)
    later = jnp.dot(hi, u, preferred_element_type=F32) + jnp.dot(lo, u, preferred_element_type=F32)
    return log_beta + later, jnp.sum(log_fail, axis=1, keepdims=True)


def _sb_kernel(q_ref, k_ref, v_ref, u_ref, o_ref, acc_ref, carry_ref, *, blk, heads):
    qi = pl.program_id(2)
    u = u_ref[...]

    def wide(c):
        return jnp.concatenate([c] * (blk // LANES), axis=1)

    def kv(g, kb):
        start = pl.multiple_of(kb * blk, blk)
        return k_ref[g, pl.ds(start, blk), :], v_ref[g, pl.ds(start, blk), :]

    def pv(w, v):
        return jnp.dot(w.astype(BF16), v, preferred_element_type=F32)

    row = lax.broadcasted_iota(jnp.int32, (blk, blk), 0)
    col = lax.broadcasted_iota(jnp.int32, (blk, blk), 1)
    mask = col < row
    no_prev = jnp.where(qi > 0, 0.0, -jnp.inf)
    remaining = []
    for g in range(heads):
        q = q_ref[g]
        k_a, v_a = kv(g, qi)
        k_b, v_b = kv(g, jnp.maximum(qi - 1, 0))
        logw_a, total_a = _sb_chunk(q, k_a, u, mask, blk)
        logw_b, total_b = _sb_chunk(q, k_b, u, None, blk)
        w_a = jnp.where(mask, jnp.exp2(logw_a), 0.0)
        w_b = jnp.exp2(logw_b + (total_a + no_prev))
        acc_ref[g] = pv(w_a, v_a) + pv(w_b, v_b)
        carry = total_a + total_b
        carry_ref[g] = jnp.broadcast_to(carry, (blk, LANES))
        remaining.append(jnp.max(carry))

    for g in range(heads):
        q = q_ref[g]

        def body(state, g=g, q=q):
            kb, _ = state
            k, v = kv(g, kb)
            logw, total = _sb_chunk(q, k, u, None, blk)
            carry = carry_ref[g]
            acc_ref[g] += pv(jnp.exp2(logw + wide(carry)), v)
            carry = carry + total
            carry_ref[g] = carry
            return kb - 1, jnp.max(carry)

        def cond(state):
            kb, left = state
            return jnp.logical_and(kb >= 0, left > SB_LOG2_UNDERFLOW)

        lax.while_loop(cond, body, (qi - 2, remaining[g]))
        o_ref[g] = acc_ref[g].astype(BF16)


def _sb_attn(proj, cum_mat, batch, seq):
    blk = min(256, seq)
    heads = 4
    nq = seq // blk
    t = batch * seq
    return pl.pallas_call(
        functools.partial(_sb_kernel, blk=blk, heads=heads),
        grid=(batch, SB_HEADS // heads, nq),
        in_specs=[
            pl.BlockSpec((heads, blk, LANES), lambda b, h, i: (BLK_SB_Q // heads + h, b * nq + i, 0)),
            pl.BlockSpec((heads, seq, LANES), lambda b, h, i: (BLK_SB_K // heads + h, b, 0)),
            pl.BlockSpec((heads, seq, LANES), lambda b, h, i: (BLK_SB_V // heads + h, b, 0)),
            pl.BlockSpec((blk, blk), lambda b, h, i: (0, 0)),
        ],
        out_specs=pl.BlockSpec((heads, blk, LANES), lambda b, h, i: (h, b * nq + i, 0)),
        out_shape=jax.ShapeDtypeStruct((SB_HEADS, t, LANES), BF16),
        scratch_shapes=[pltpu.VMEM((heads, blk, LANES), F32), pltpu.VMEM((heads, blk, LANES), F32)],
        compiler_params=_params(("parallel", "parallel", "arbitrary"), 48),
        name="sb_attn",
    )(proj, proj, proj, cum_mat)


def _mla_prep_kernel(cq_ref, ckv_ref, kr_ref, pos_ref, invf_ref, gq_ref, gkv_ref, wq_ref, wkv_ref,
                     q_out, k_out, v_out):
    nlat = MLA_Q_RANK // LANES
    cq = _cat_blocks(cq_ref, nlat).astype(F32)
    ckv = _cat_blocks(ckv_ref, nlat).astype(F32)
    q = jnp.dot(_rms(cq, gq_ref[...]).astype(BF16), wq_ref[...], preferred_element_type=F32)
    kv = jnp.dot(_rms(ckv, gkv_ref[...]).astype(BF16), wkv_ref[...], preferred_element_type=F32)

    half = MLA_ROPE_DIM // 2
    groups = LANES // half
    rows = pos_ref.shape[0] // groups
    lane = lax.broadcasted_iota(jnp.int32, (rows, LANES), 1)
    pos = pos_ref[...].astype(F32)
    packed = pos[(groups - 1) * rows:]
    for j in range(groups - 2, -1, -1):
        packed = jnp.where(lane < (j + 1) * half, pos[j * rows:(j + 1) * rows], packed)
    ang = packed * invf_ref[...]
    cos_p, sin_p = jnp.cos(ang), jnp.sin(ang)

    def unpack(t, j, k):
        shift = ((k - j) * half) % LANES
        return pltpu.roll(t, shift, axis=1) if shift else t

    cos_t, sin_t = [], []
    for j in range(groups):
        cos_t.append(jnp.where(lane < half, unpack(cos_p, j, 0),
                               jnp.where(lane < MLA_ROPE_DIM, unpack(cos_p, j, 1), 0.0)))
        sin_t.append(jnp.where(lane < half, -unpack(sin_p, j, 0),
                               jnp.where(lane < MLA_ROPE_DIM, unpack(sin_p, j, 1), 0.0)))
    cos_t = jnp.concatenate(cos_t, axis=0)
    sin_t = jnp.concatenate(sin_t, axis=0)

    def rope(a):
        return a * cos_t + pltpu.roll(a, half, axis=1) * sin_t

    k_rope = rope(kr_ref[...].astype(F32))
    scale = MLA_QK_DIM ** -0.5 * LOG2E
    for h in range(MLA_HEADS):
        lo, hi = h * LANES, (h + 1) * LANES
        q_nope = q[:, lo:hi] * scale
        q_rope = rope(q[:, MLA_HEADS * LANES + lo:MLA_HEADS * LANES + hi]) * scale
        q_out[h] = jnp.concatenate([q_nope, q_rope], axis=1).astype(BF16)
        k_out[h] = jnp.concatenate([kv[:, lo:hi], k_rope], axis=1).astype(BF16)
        v_out[h] = kv[:, MLA_HEADS * LANES + lo:MLA_HEADS * LANES + hi].astype(BF16)


def _mla_prep(proj, kr, pos, invf, g_cq, g_ckv, wq, wkv):
    t = proj.shape[1]
    tm = min(512, t)
    nlat = MLA_Q_RANK // LANES
    const = lambda i: (0, 0)
    return pl.pallas_call(
        _mla_prep_kernel,
        grid=(t // tm,),
        in_specs=[
            pl.BlockSpec((nlat, tm, LANES), lambda i: (BLK_CQ // nlat, i, 0)),
            pl.BlockSpec((nlat, tm, LANES), lambda i: (BLK_CKV // nlat, i, 0)),
            pl.BlockSpec((tm, LANES), lambda i: (i, 0)),
            pl.BlockSpec((tm, 1), lambda i: (i, 0)),
            pl.BlockSpec((1, LANES), const),
            pl.BlockSpec((1, MLA_Q_RANK), const),
            pl.BlockSpec((1, MLA_KV_RANK), const),
            pl.BlockSpec(wq.shape, const),
            pl.BlockSpec(wkv.shape, const),
        ],
        out_specs=[
            pl.BlockSpec((MLA_HEADS, tm, MLA_QK_PAD), lambda i: (0, i, 0)),
            pl.BlockSpec((MLA_HEADS, tm, MLA_QK_PAD), lambda i: (0, i, 0)),
            pl.BlockSpec((MLA_HEADS, tm, LANES), lambda i: (0, i, 0)),
        ],
        out_shape=[
            jax.ShapeDtypeStruct((MLA_HEADS, t, MLA_QK_PAD), BF16),
            jax.ShapeDtypeStruct((MLA_HEADS, t, MLA_QK_PAD), BF16),
            jax.ShapeDtypeStruct((MLA_HEADS, t, LANES), BF16),
        ],
        compiler_params=_params(("parallel",), 48),
        name="mla_prep",
    )(proj, proj, kr, pos, invf, g_cq, g_ckv, wq, wkv)


def _mla_kernel(q_ref, k_ref, v_ref, o_ref, m_ref, acc_ref, *, blk, heads, subs):
    qi = pl.program_id(2)
    m_ref[...] = jnp.full_like(m_ref, -jnp.inf)
    acc_ref[...] = jnp.zeros_like(acc_ref)

    def step(kb, kinds):
        start = pl.multiple_of(kb * blk, blk)
        for g in range(heads):
            k = k_ref[g, pl.ds(start, blk), :]
            v = v_ref[g, pl.ds(start, blk), :]
            v_ext = jnp.concatenate([v, jnp.ones_like(v)], axis=1)
            for r, kind in enumerate(kinds):
                if kind is None:
                    continue
                rows = slice(r * blk, (r + 1) * blk)
                s = lax.dot_general(q_ref[g, rows, :], k, (((1,), (1,)), ((), ())),
                                    preferred_element_type=F32)
                if kind == "diag":
                    row = lax.broadcasted_iota(jnp.int32, (blk, blk), 0)
                    col = lax.broadcasted_iota(jnp.int32, (blk, blk), 1)
                    s = jnp.where(col <= row, s, -jnp.inf)
                m_prev = m_ref[g, rows, :]
                m_new = jnp.maximum(m_prev, jnp.max(s, axis=1, keepdims=True))
                alpha = jnp.exp2(m_prev - m_new)
                p = jnp.exp2(s - jnp.concatenate([m_new] * (blk // LANES), axis=1))
                acc_ref[g, rows, :] = (jnp.concatenate([alpha, alpha], axis=1) * acc_ref[g, rows, :]
                                       + jnp.dot(p.astype(BF16), v_ext, preferred_element_type=F32))
                m_ref[g, rows, :] = m_new

    def below_diagonal(j, c):
        for d in range(subs):
            step(subs * j + d, ("full",) * subs)
        return c

    lax.fori_loop(0, qi, below_diagonal, 0)
    for d in range(subs):
        step(subs * qi + d, tuple("diag" if r == d else ("full" if r > d else None)
                                  for r in range(subs)))
    for g in range(heads):
        acc = acc_ref[g]
        o_ref[g] = (acc[:, :LANES] / acc[:, LANES:]).astype(BF16)


def _mla_attn(q, k, v, batch, seq):
    blk = min(512, seq)
    subs = min(2, seq // blk)
    heads = 4
    tq = subs * blk
    assert seq % tq == 0, "sequence must tile into MLA query tiles"
    nq = seq // tq
    t = batch * seq
    resident = pl.Buffered(1)
    return pl.pallas_call(
        functools.partial(_mla_kernel, blk=blk, heads=heads, subs=subs),
        grid=(batch, MLA_HEADS // heads, nq),
        in_specs=[
            pl.BlockSpec((heads, tq, MLA_QK_PAD), lambda b, h, i: (h, b * nq + i, 0)),
            pl.BlockSpec((heads, seq, MLA_QK_PAD), lambda b, h, i: (h, b, 0), pipeline_mode=resident),
            pl.BlockSpec((heads, seq, LANES), lambda b, h, i: (h, b, 0), pipeline_mode=resident),
        ],
        out_specs=pl.BlockSpec((heads, tq, LANES), lambda b, h, i: (h, b * nq + i, 0)),
        out_shape=jax.ShapeDtypeStruct((MLA_HEADS, t, LANES), BF16),
        scratch_shapes=[pltpu.VMEM((heads, tq, LANES), F32),
                        pltpu.VMEM((heads, tq, 2 * LANES), F32)],
        compiler_params=_params(("parallel", "parallel", "arbitrary"), 52),
        name="mla_attn",
    )(q, k, v)


def _mix_kernel(x_ref, osb_ref, omla_ref, gsb_ref, gmla_ref, wsb_ref, wmla_ref, wout_ref,
                gpost_ref, gpre_ref, x1_ref, h2_ref):
    nd = D_MODEL // LANES
    tm = x_ref.shape[0]
    half = tm // 2
    for rows in (slice(0, half), slice(half, tm)):
        cat = lambda ref, n: jnp.concatenate([ref[c, rows, :] for c in range(n)], axis=1)
        a = jnp.dot(cat(osb_ref, SB_HEADS), wsb_ref[...], preferred_element_type=F32)
        b = jnp.dot(cat(omla_ref, MLA_HEADS), wmla_ref[...], preferred_element_type=F32)
        gate_sb = cat(gsb_ref, nd).astype(F32)
        gate_mla = cat(gmla_ref, nd).astype(F32)
        mixed = _sigmoid(gate_sb) * a + _sigmoid(gate_mla) * b
        y = jnp.dot(mixed.astype(BF16), wout_ref[...], preferred_element_type=F32)
        x1 = x_ref[rows, :] + _rms(y, gpost_ref[...])
        x1_ref[rows, :] = x1
        h2_ref[rows, :] = _rms(x1, gpre_ref[...]).astype(BF16)


def _mix(x2d, o_sb, o_mla, proj, w_sb_o, w_mla_o, w_out, g_post, g_pre):
    t, d = x2d.shape
    tm = min(512, t)
    nd = d // LANES
    const = lambda i: (0, 0)
    single = pl.Buffered(1)
    return pl.pallas_call(
        _mix_kernel,
        grid=(t // tm,),
        in_specs=[
            pl.BlockSpec((tm, d), lambda i: (i, 0)),
            pl.BlockSpec((SB_HEADS, tm, LANES), lambda i: (0, i, 0)),
            pl.BlockSpec((MLA_HEADS, tm, LANES), lambda i: (0, i, 0)),
            pl.BlockSpec((nd, tm, LANES), lambda i: (BLK_GATE_SB // nd, i, 0)),
            pl.BlockSpec((nd, tm, LANES), lambda i: (BLK_GATE_MLA // nd, i, 0)),
            pl.BlockSpec(w_sb_o.shape, const, pipeline_mode=single),
            pl.BlockSpec(w_mla_o.shape, const, pipeline_mode=single),
            pl.BlockSpec(w_out.shape, const, pipeline_mode=single),
            pl.BlockSpec((1, d), const),
            pl.BlockSpec((1, d), const),
        ],
        out_specs=[
            pl.BlockSpec((tm, d), lambda i: (i, 0)),
            pl.BlockSpec((tm, d), lambda i: (i, 0)),
        ],
        out_shape=[
            jax.ShapeDtypeStruct((t, d), F32),
            jax.ShapeDtypeStruct((t, d), BF16),
        ],
        compiler_params=_params(("parallel",), 56),
        name="mix",
    )(x2d, o_sb, o_mla, proj, proj, w_sb_o, w_mla_o, w_out, g_post, g_pre)


def _mlp_kernel(h_ref, wup_ref, wdown_ref, g_ref, o_ref):
    f = pl.program_id(1)

    @pl.when(f == 0)
    def _():
        o_ref[...] = jnp.zeros_like(o_ref)

    u = jnp.dot(h_ref[...], wup_ref[0], preferred_element_type=F32)
    u = jnp.square(jnp.maximum(u, 0.0))
    o_ref[...] += jnp.dot(u.astype(BF16), wdown_ref[...], preferred_element_type=F32)

    @pl.when(f == pl.num_programs(1) - 1)
    def _():
        o_ref[...] = _rms(o_ref[...], g_ref[...])


def _mlp(h2, w_up, w_down, g_post):
    t, d = h2.shape
    nf, _, tf = w_up.shape
    dff = nf * tf
    tm = min(1024, t)
    return pl.pallas_call(
        _mlp_kernel,
        grid=(t // tm, dff // tf),
        in_specs=[
            pl.BlockSpec((tm, d), lambda i, f: (i, 0)),
            pl.BlockSpec((1, d, tf), lambda i, f: (f, 0, 0)),
            pl.BlockSpec((tf, d), lambda i, f: (f, 0)),
            pl.BlockSpec((1, d), lambda i, f: (0, 0)),
        ],
        out_specs=pl.BlockSpec((tm, d), lambda i, f: (i, 0)),
        out_shape=jax.ShapeDtypeStruct((t, d), F32),
        compiler_params=_params(("parallel", "arbitrary"), 58),
        name="mlp",
    )(h2, w_up, w_down, g_post)


def _ple_kernel(x1_ref, r_ref, p_ref, wple_ref, wgate_ref, g_ref, o_ref):
    x = x1_ref[...] + r_ref[...]
    e = _rms(jnp.dot(p_ref[...].astype(BF16), wple_ref[...], preferred_element_type=F32), g_ref[...])
    gate = _sigmoid(jnp.dot(x.astype(BF16), wgate_ref[...], preferred_element_type=F32))
    o_ref[...] = x + gate * e


def _ple(x1, r, p2d, w_ple, w_gate, g_ple):
    t, d = x1.shape
    tm = min(512, t)
    const = lambda i: (0, 0)
    single = pl.Buffered(1)
    return pl.pallas_call(
        _ple_kernel,
        grid=(t // tm,),
        in_specs=[
            pl.BlockSpec((tm, d), lambda i: (i, 0)),
            pl.BlockSpec((tm, d), lambda i: (i, 0)),
            pl.BlockSpec((tm, p2d.shape[1]), lambda i: (i, 0)),
            pl.BlockSpec(w_ple.shape, const, pipeline_mode=single),
            pl.BlockSpec(w_gate.shape, const, pipeline_mode=single),
            pl.BlockSpec((1, d), const),
        ],
        out_specs=pl.BlockSpec((tm, d), lambda i: (i, 0)),
        out_shape=jax.ShapeDtypeStruct((t, d), F32),
        compiler_params=_params(("parallel",), 48),
        name="ple",
    )(x1, r, p2d, w_ple, w_gate, g_ple)


def _reorder_w_in(w):
    o_kr = 3 * SB_WIDTH + MLA_Q_RANK + MLA_KV_RANK
    o_gate = o_kr + MLA_ROPE_DIM
    k_rope = w[:, o_kr:o_gate]
    sb_q = w[:, :SB_WIDTH] * (SB_HEAD_DIM ** -0.5 * LOG2E)
    main = jnp.concatenate([sb_q, w[:, SB_WIDTH:o_kr], w[:, o_gate:]], axis=1).astype(BF16)
    main = main.reshape(w.shape[0], -1, IN_TILE_BLOCKS * LANES).transpose(1, 0, 2)
    return main, jnp.concatenate([k_rope, k_rope], axis=1).astype(BF16)


def _reorder_w_q_up(w):
    w3 = w.reshape(MLA_Q_RANK, MLA_HEADS, MLA_QK_DIM)
    nope = w3[:, :, :MLA_NOPE_DIM].reshape(MLA_Q_RANK, MLA_HEADS * MLA_NOPE_DIM)
    rope = w3[:, :, MLA_NOPE_DIM:]
    rope = jnp.concatenate([rope, rope], axis=-1).reshape(MLA_Q_RANK, MLA_HEADS * LANES)
    return jnp.concatenate([nope, rope], axis=1).astype(BF16)


def _reorder_w_kv_up(w):
    w3 = w.reshape(MLA_KV_RANK, MLA_HEADS, MLA_NOPE_DIM + MLA_V_DIM)
    k_nope = w3[:, :, :MLA_NOPE_DIM].reshape(MLA_KV_RANK, MLA_HEADS * MLA_NOPE_DIM)
    v = w3[:, :, MLA_NOPE_DIM:].reshape(MLA_KV_RANK, MLA_HEADS * MLA_V_DIM)
    return jnp.concatenate([k_nope, v], axis=1).astype(BF16)


def _cumsum_matrix(blk):
    j = jnp.arange(blk)[:, None]
    s = jnp.arange(blk)[None, :]
    return (j > s).astype(BF16)


def _layer(x2d, p2d, pos, invf, batch, seq, g_pre_mix, w_in, g_cq, g_ckv, w_q_up, w_kv_up,
           w_sb_o, w_mla_o, w_out, g_post_mix, g_pre_mlp, w_up, w_down, g_post_mlp,
           w_ple, g_ple, w_ple_gate):
    row = lambda g: g.reshape(1, -1).astype(F32)
    proj, kr = _in_proj(x2d, row(g_pre_mix), *_reorder_w_in(w_in))
    o_sb = _sb_attn(proj, _cumsum_matrix(min(256, seq)), batch, seq)
    q, k, v = _mla_prep(proj, kr, pos, invf, row(g_cq), row(g_ckv),
                        _reorder_w_q_up(w_q_up), _reorder_w_kv_up(w_kv_up))
    o_mla = _mla_attn(q, k, v, batch, seq)
    x1, h2 = _mix(x2d, o_sb, o_mla, proj, w_sb_o.astype(BF16), w_mla_o.astype(BF16),
                  w_out.astype(BF16), row(g_post_mix), row(g_pre_mlp))
    w_up_tiles = w_up.astype(BF16).reshape(w_up.shape[0], -1, MLP_TF).transpose(1, 0, 2)
    r = _mlp(h2, w_up_tiles, w_down.astype(BF16), row(g_post_mlp))
    return _ple(x1, r, p2d, w_ple.astype(BF16), w_ple_gate.astype(BF16), row(g_ple))


def kernel(x, p, positions, g_pre_mix, w_in, g_cq, g_ckv, w_q_up, w_kv_up, w_sb_o, w_mla_o, w_out, g_post_mix, g_pre_mlp, w_up, w_down, g_post_mlp, w_ple, g_ple, w_ple_gate):
    batch, seq, d = x.shape
    assert d == D_MODEL and seq % 256 == 0, "sequence must tile into stick-breaking blocks"
    t = batch * seq
    half = MLA_ROPE_DIM // 2
    inv_freq = ROPE_THETA ** (-jnp.arange(half, dtype=F32) / half)
    invf = jnp.tile(inv_freq, LANES // half).reshape(1, LANES)
    pos = positions.reshape(t, 1).astype(jnp.int32)
    x2d = x.reshape(t, d)
    for i in range(p.shape[0]):
        x2d = _layer(x2d, p[i].reshape(t, PLE_DIM), pos, invf, batch, seq, g_pre_mix[i], w_in[i],
                     g_cq[i], g_ckv[i], w_q_up[i], w_kv_up[i], w_sb_o[i], w_mla_o[i], w_out[i],
                     g_post_mix[i], g_pre_mlp[i], w_up[i], w_down[i], g_post_mlp[i],
                     w_ple[i], g_ple[i], w_ple_gate[i])
    return x2d.reshape(batch, seq, d)
```

```python
import functools

import jax
import jax.numpy as jnp
from jax import lax
from jax.experimental import pallas as pl
from jax.experimental.pallas import tpu as pltpu

F32 = jnp.float32
BF16 = jnp.bfloat16

D_MODEL = 2048
PLE_DIM = 256
SB_HEADS = 8
SB_HEAD_DIM = 128
MLA_HEADS = 8
MLA_NOPE_DIM = 128
MLA_ROPE_DIM = 64
MLA_V_DIM = 128
MLA_Q_RANK = 512
MLA_KV_RANK = 512
D_FF = 4 * D_MODEL
ROPE_THETA = 10000.0
EPS = 1e-6
SB_WIDTH = SB_HEADS * SB_HEAD_DIM
MLA_WIDTH = MLA_HEADS * MLA_V_DIM
MLA_QK_DIM = MLA_NOPE_DIM + MLA_ROPE_DIM
MLA_QK_PAD = 256

LANES = 128
MIB = 1024 * 1024

BLK_SB_Q = 0
BLK_SB_K = 8
BLK_SB_V = 16
BLK_CQ = 24
BLK_CKV = 28
BLK_GATE_SB = 32
BLK_GATE_MLA = 48
IN_TILE_BLOCKS = 16
MLP_TF = 1024

LOG2E = 1.4426950408889634

SB_LOG2_UNDERFLOW = -104.0 * LOG2E


def _params(semantics, vmem_mib):
    return pltpu.CompilerParams(dimension_semantics=semantics, vmem_limit_bytes=vmem_mib * MIB)


def _rms(xf, g):
    y = xf * lax.rsqrt(jnp.mean(xf * xf, axis=-1, keepdims=True) + EPS)
    return y * g


def _sigmoid(x):
    return 1.0 / (1.0 + jnp.exp(-x))


def _cat_blocks(ref, n):
    return jnp.concatenate([ref[c] for c in range(n)], axis=1)


def _in_proj_kernel(x_ref, g_ref, w_ref, wkr_ref, o_ref, kr_ref, h_ref, *, nblk):
    @pl.when(pl.program_id(1) == 0)
    def _():
        h = _rms(x_ref[...], g_ref[...]).astype(BF16)
        h_ref[...] = h
        kr_ref[...] = jnp.dot(h, wkr_ref[...], preferred_element_type=F32).astype(BF16)

    acc = jnp.dot(h_ref[...], w_ref[...], preferred_element_type=F32)
    for c in range(nblk):
        o_ref[c] = acc[:, c * LANES:(c + 1) * LANES].astype(BF16)


def _in_proj(x2d, g, w, w_kr):
    t, d = x2d.shape
    n = w.shape[1]
    tn = IN_TILE_BLOCKS * LANES
    tm = min(1024, t)
    return pl.pallas_call(
        functools.partial(_in_proj_kernel, nblk=IN_TILE_BLOCKS),
        grid=(t // tm, n // tn),
        in_specs=[
            pl.BlockSpec((tm, d), lambda i, j: (i, 0)),
            pl.BlockSpec((1, d), lambda i, j: (0, 0)),
            pl.BlockSpec((d, tn), lambda i, j: (0, j)),
            pl.BlockSpec((d, LANES), lambda i, j: (0, 0)),
        ],
        out_specs=[
            pl.BlockSpec((IN_TILE_BLOCKS, tm, LANES), lambda i, j: (j, i, 0)),
            pl.BlockSpec((tm, LANES), lambda i, j: (i, 0)),
        ],
        out_shape=[
            jax.ShapeDtypeStruct((n // LANES, t, LANES), BF16),
            jax.ShapeDtypeStruct((t, LANES), BF16),
        ],
        scratch_shapes=[pltpu.VMEM((tm, d), BF16)],
        compiler_params=_params(("parallel", "arbitrary"), 58),
        name="in_proj",
    )(x2d, g, w, w_kr)


def _sb_chunk(q, k, u, mask, blk):
    z = lax.dot_general(q, k, (((1,), (1,)), ((), ())), preferred_element_type=F32)
    sign_bit = jnp.uint32(0x80000000)
    neg_abs = lax.bitcast_convert_type(lax.bitcast_convert_type(z, jnp.uint32) | sign_bit, F32)
    c = jnp.log(1.0 + jnp.exp2(neg_abs)) * LOG2E
    neg_part = jnp.minimum(z, 0.0)
    log_beta = neg_part - c
    log_fail = (neg_abs - neg_part) - c
    if mask is not None:
        log_fail = jnp.where(mask, log_fail, 0.0)
    hi = log_fail.astype(BF16)
    lo = (log_fail - hi.astype(F32)).astype(BF16)
    later = jnp.dot(hi, u, preferred_element_type=F32) + jnp.dot(lo, u, preferred_element_type=F32)
    return log_beta + later, jnp.sum(log_fail, axis=1, keepdims=True)


def _sb_kernel(q_ref, k_ref, v_ref, u_ref, o_ref, acc_ref, carry_ref, *, blk, heads):
    qi = pl.program_id(2)
    u = u_ref[...]

    def wide(c):
        return jnp.concatenate([c] * (blk // LANES), axis=1)

    def kv(g, kb):
        start = pl.multiple_of(kb * blk, blk)
        return k_ref[g, pl.ds(start, blk), :], v_ref[g, pl.ds(start, blk), :]

    def pv(w, v):
        return jnp.dot(w.astype(BF16), v, preferred_element_type=F32)

    row = lax.broadcasted_iota(jnp.int32, (blk, blk), 0)
    col = lax.broadcasted_iota(jnp.int32, (blk, blk), 1)
    mask = col < row
    no_prev = jnp.where(qi > 0, 0.0, -jnp.inf)
    remaining = []
    for g in range(heads):
        q = q_ref[g]
        k_a, v_a = kv(g, qi)
        k_b, v_b = kv(g, jnp.maximum(qi - 1, 0))
        logw_a, total_a = _sb_chunk(q, k_a, u, mask, blk)
        logw_b, total_b = _sb_chunk(q, k_b, u, None, blk)
        w_a = jnp.where(mask, jnp.exp2(logw_a), 0.0)
        w_b = jnp.exp2(logw_b + (total_a + no_prev))
        acc_ref[g] = pv(w_a, v_a) + pv(w_b, v_b)
        carry = total_a + total_b
        carry_ref[g] = jnp.broadcast_to(carry, (blk, LANES))
        remaining.append(jnp.max(carry))

    for g in range(heads):
        q = q_ref[g]

        def body(state, g=g, q=q):
            kb, _ = state
            k, v = kv(g, kb)
            logw, total = _sb_chunk(q, k, u, None, blk)
            carry = carry_ref[g]
            acc_ref[g] += pv(jnp.exp2(logw + wide(carry)), v)
            carry = carry + total
            carry_ref[g] = carry
            return kb - 1, jnp.max(carry)

        def cond(state):
            kb, left = state
            return jnp.logical_and(kb >= 0, left > SB_LOG2_UNDERFLOW)

        lax.while_loop(cond, body, (qi - 2, remaining[g]))
        o_ref[g] = acc_ref[g].astype(BF16)


def _sb_attn(proj, cum_mat, batch, seq):
    blk = min(256, seq)
    heads = 4
    nq = seq // blk
    t = batch * seq
    return pl.pallas_call(
        functools.partial(_sb_kernel, blk=blk, heads=heads),
        grid=(batch, SB_HEADS // heads, nq),
        in_specs=[
            pl.BlockSpec((heads, blk, LANES), lambda b, h, i: (BLK_SB_Q // heads + h, b * nq + i, 0)),
            pl.BlockSpec((heads, seq, LANES), lambda b, h, i: (BLK_SB_K // heads + h, b, 0)),
            pl.BlockSpec((heads, seq, LANES), lambda b, h, i: (BLK_SB_V // heads + h, b, 0)),
            pl.BlockSpec((blk, blk), lambda b, h, i: (0, 0)),
        ],
        out_specs=pl.BlockSpec((heads, blk, LANES), lambda b, h, i: (h, b * nq + i, 0)),
        out_shape=jax.ShapeDtypeStruct((SB_HEADS, t, LANES), BF16),
        scratch_shapes=[pltpu.VMEM((heads, blk, LANES), F32), pltpu.VMEM((heads, blk, LANES), F32)],
        compiler_params=_params(("parallel", "parallel", "arbitrary"), 48),
        name="sb_attn",
    )(proj, proj, proj, cum_mat)


def _mla_prep_kernel(cq_ref, ckv_ref, kr_ref, pos_ref, invf_ref, gq_ref, gkv_ref, wq_ref, wkv_ref,
                     q_out, k_out, v_out):
    nlat = MLA_Q_RANK // LANES
    cq = _cat_blocks(cq_ref, nlat).astype(F32)
    ckv = _cat_blocks(ckv_ref, nlat).astype(F32)
    q = jnp.dot(_rms(cq, gq_ref[...]).astype(BF16), wq_ref[...], preferred_element_type=F32)
    kv = jnp.dot(_rms(ckv, gkv_ref[...]).astype(BF16), wkv_ref[...], preferred_element_type=F32)

    half = MLA_ROPE_DIM // 2
    groups = LANES // half
    rows = pos_ref.shape[0] // groups
    lane = lax.broadcasted_iota(jnp.int32, (rows, LANES), 1)
    pos = pos_ref[...].astype(F32)
    packed = pos[(groups - 1) * rows:]
    for j in range(groups - 2, -1, -1):
        packed = jnp.where(lane < (j + 1) * half, pos[j * rows:(j + 1) * rows], packed)
    ang = packed * invf_ref[...]
    cos_p, sin_p = jnp.cos(ang), jnp.sin(ang)

    def unpack(t, j, k):
        shift = ((k - j) * half) % LANES
        return pltpu.roll(t, shift, axis=1) if shift else t

    cos_t, sin_t = [], []
    for j in range(groups):
        cos_t.append(jnp.where(lane < half, unpack(cos_p, j, 0),
                               jnp.where(lane < MLA_ROPE_DIM, unpack(cos_p, j, 1), 0.0)))
        sin_t.append(jnp.where(lane < half, -unpack(sin_p, j, 0),
                               jnp.where(lane < MLA_ROPE_DIM, unpack(sin_p, j, 1), 0.0)))
    cos_t = jnp.concatenate(cos_t, axis=0)
    sin_t = jnp.concatenate(sin_t, axis=0)

    def rope(a):
        return a * cos_t + pltpu.roll(a, half, axis=1) * sin_t

    k_rope = rope(kr_ref[...].astype(F32))
    scale = MLA_QK_DIM ** -0.5 * LOG2E
    for h in range(MLA_HEADS):
        lo, hi = h * LANES, (h + 1) * LANES
        q_nope = q[:, lo:hi] * scale
        q_rope = rope(q[:, MLA_HEADS * LANES + lo:MLA_HEADS * LANES + hi]) * scale
        q_out[h] = jnp.concatenate([q_nope, q_rope], axis=1).astype(BF16)
        k_out[h] = jnp.concatenate([kv[:, lo:hi], k_rope], axis=1).astype(BF16)
        v_out[h] = kv[:, MLA_HEADS * LANES + lo:MLA_HEADS * LANES + hi].astype(BF16)


def _mla_prep(proj, kr, pos, invf, g_cq, g_ckv, wq, wkv):
    t = proj.shape[1]
    tm = min(512, t)
    nlat = MLA_Q_RANK // LANES
    const = lambda i: (0, 0)
    return pl.pallas_call(
        _mla_prep_kernel,
        grid=(t // tm,),
        in_specs=[
            pl.BlockSpec((nlat, tm, LANES), lambda i: (BLK_CQ // nlat, i, 0)),
            pl.BlockSpec((nlat, tm, LANES), lambda i: (BLK_CKV // nlat, i, 0)),
            pl.BlockSpec((tm, LANES), lambda i: (i, 0)),
            pl.BlockSpec((tm, 1), lambda i: (i, 0)),
            pl.BlockSpec((1, LANES), const),
            pl.BlockSpec((1, MLA_Q_RANK), const),
            pl.BlockSpec((1, MLA_KV_RANK), const),
            pl.BlockSpec(wq.shape, const),
            pl.BlockSpec(wkv.shape, const),
        ],
        out_specs=[
            pl.BlockSpec((MLA_HEADS, tm, MLA_QK_PAD), lambda i: (0, i, 0)),
            pl.BlockSpec((MLA_HEADS, tm, MLA_QK_PAD), lambda i: (0, i, 0)),
            pl.BlockSpec((MLA_HEADS, tm, LANES), lambda i: (0, i, 0)),
        ],
        out_shape=[
            jax.ShapeDtypeStruct((MLA_HEADS, t, MLA_QK_PAD), BF16),
            jax.ShapeDtypeStruct((MLA_HEADS, t, MLA_QK_PAD), BF16),
            jax.ShapeDtypeStruct((MLA_HEADS, t, LANES), BF16),
        ],
        compiler_params=_params(("parallel",), 48),
        name="mla_prep",
    )(proj, proj, kr, pos, invf, g_cq, g_ckv, wq, wkv)


def _mla_kernel(q_ref, k_ref, v_ref, o_ref, m_ref, acc_ref, *, blk, heads, subs):
    qi = pl.program_id(2)
    m_ref[...] = jnp.full_like(m_ref, -jnp.inf)
    acc_ref[...] = jnp.zeros_like(acc_ref)

    def step(kb, kinds):
        start = pl.multiple_of(kb * blk, blk)
        for g in range(heads):
            k = k_ref[g, pl.ds(start, blk), :]
            v = v_ref[g, pl.ds(start, blk), :]
            v_ext = jnp.concatenate([v, jnp.ones_like(v)], axis=1)
            for r, kind in enumerate(kinds):
                if kind is None:
                    continue
                rows = slice(r * blk, (r + 1) * blk)
                s = lax.dot_general(q_ref[g, rows, :], k, (((1,), (1,)), ((), ())),
                                    preferred_element_type=F32)
                if kind == "diag":
                    row = lax.broadcasted_iota(jnp.int32, (blk, blk), 0)
                    col = lax.broadcasted_iota(jnp.int32, (blk, blk), 1)
                    s = jnp.where(col <= row, s, -jnp.inf)
                m_prev = m_ref[g, rows, :]
                m_new = jnp.maximum(m_prev, jnp.max(s, axis=1, keepdims=True))
                alpha = jnp.exp2(m_prev - m_new)
                p = jnp.exp2(s - jnp.concatenate([m_new] * (blk // LANES), axis=1))
                acc_ref[g, rows, :] = (jnp.concatenate([alpha, alpha], axis=1) * acc_ref[g, rows, :]
                                       + jnp.dot(p.astype(BF16), v_ext, preferred_element_type=F32))
                m_ref[g, rows, :] = m_new

    def below_diagonal(j, c):
        for d in range(subs):
            step(subs * j + d, ("full",) * subs)
        return c

    lax.fori_loop(0, qi, below_diagonal, 0)
    for d in range(subs):
        step(subs * qi + d, tuple("diag" if r == d else ("full" if r > d else None)
                                  for r in range(subs)))
    for g in range(heads):
        acc = acc_ref[g]
        o_ref[g] = (acc[:, :LANES] / acc[:, LANES:]).astype(BF16)


def _mla_attn(q, k, v, batch, seq):
    blk = min(512, seq)
    subs = min(2, seq // blk)
    heads = 4
    tq = subs * blk
    assert seq % tq == 0, "sequence must tile into MLA query tiles"
    nq = seq // tq
    t = batch * seq
    resident = pl.Buffered(1)
    return pl.pallas_call(
        functools.partial(_mla_kernel, blk=blk, heads=heads, subs=subs),
        grid=(batch, MLA_HEADS // heads, nq),
        in_specs=[
            pl.BlockSpec((heads, tq, MLA_QK_PAD), lambda b, h, i: (h, b * nq + i, 0)),
            pl.BlockSpec((heads, seq, MLA_QK_PAD), lambda b, h, i: (h, b, 0), pipeline_mode=resident),
            pl.BlockSpec((heads, seq, LANES), lambda b, h, i: (h, b, 0), pipeline_mode=resident),
        ],
        out_specs=pl.BlockSpec((heads, tq, LANES), lambda b, h, i: (h, b * nq + i, 0)),
        out_shape=jax.ShapeDtypeStruct((MLA_HEADS, t, LANES), BF16),
        scratch_shapes=[pltpu.VMEM((heads, tq, LANES), F32),
                        pltpu.VMEM((heads, tq, 2 * LANES), F32)],
        compiler_params=_params(("parallel", "parallel", "arbitrary"), 52),
        name="mla_attn",
    )(q, k, v)


def _mix_kernel(x_ref, osb_ref, omla_ref, gsb_ref, gmla_ref, wsb_ref, wmla_ref, wout_ref,
                gpost_ref, gpre_ref, x1_ref, h2_ref):
    nd = D_MODEL // LANES
    tm = x_ref.shape[0]
    half = tm // 2
    for rows in (slice(0, half), slice(half, tm)):
        cat = lambda ref, n: jnp.concatenate([ref[c, rows, :] for c in range(n)], axis=1)
        a = jnp.dot(cat(osb_ref, SB_HEADS), wsb_ref[...], preferred_element_type=F32)
        b = jnp.dot(cat(omla_ref, MLA_HEADS), wmla_ref[...], preferred_element_type=F32)
        gate_sb = cat(gsb_ref, nd).astype(F32)
        gate_mla = cat(gmla_ref, nd).astype(F32)
        mixed = _sigmoid(gate_sb) * a + _sigmoid(gate_mla) * b
        y = jnp.dot(mixed.astype(BF16), wout_ref[...], preferred_element_type=F32)
        x1 = x_ref[rows, :] + _rms(y, gpost_ref[...])
        x1_ref[rows, :] = x1
        h2_ref[rows, :] = _rms(x1, gpre_ref[...]).astype(BF16)


def _mix(x2d, o_sb, o_mla, proj, w_sb_o, w_mla_o, w_out, g_post, g_pre):
    t, d = x2d.shape
    tm = min(512, t)
    nd = d // LANES
    const = lambda i: (0, 0)
    single = pl.Buffered(1)
    return pl.pallas_call(
        _mix_kernel,
        grid=(t // tm,),
        in_specs=[
            pl.BlockSpec((tm, d), lambda i: (i, 0)),
            pl.BlockSpec((SB_HEADS, tm, LANES), lambda i: (0, i, 0)),
            pl.BlockSpec((MLA_HEADS, tm, LANES), lambda i: (0, i, 0)),
            pl.BlockSpec((nd, tm, LANES), lambda i: (BLK_GATE_SB // nd, i, 0)),
            pl.BlockSpec((nd, tm, LANES), lambda i: (BLK_GATE_MLA // nd, i, 0)),
            pl.BlockSpec(w_sb_o.shape, const, pipeline_mode=single),
            pl.BlockSpec(w_mla_o.shape, const, pipeline_mode=single),
            pl.BlockSpec(w_out.shape, const, pipeline_mode=single),
            pl.BlockSpec((1, d), const),
            pl.BlockSpec((1, d), const),
        ],
        out_specs=[
            pl.BlockSpec((tm, d), lambda i: (i, 0)),
            pl.BlockSpec((tm, d), lambda i: (i, 0)),
        ],
        out_shape=[
            jax.ShapeDtypeStruct((t, d), F32),
            jax.ShapeDtypeStruct((t, d), BF16),
        ],
        compiler_params=_params(("parallel",), 56),
        name="mix",
    )(x2d, o_sb, o_mla, proj, proj, w_sb_o, w_mla_o, w_out, g_post, g_pre)


def _mlp_kernel(h_ref, wup_ref, wdown_ref, g_ref, o_ref):
    f = pl.program_id(1)

    @pl.when(f == 0)
    def _():
        o_ref[...] = jnp.zeros_like(o_ref)

    u = jnp.dot(h_ref[...], wup_ref[...], preferred_element_type=F32)
    u = jnp.square(jnp.maximum(u, 0.0))
    o_ref[...] += jnp.dot(u.astype(BF16), wdown_ref[...], preferred_element_type=F32)

    @pl.when(f == pl.num_programs(1) - 1)
    def _():
        o_ref[...] = _rms(o_ref[...], g_ref[...])


def _mlp(h2, w_up, w_down, g_post):
    t, d = h2.shape
    dff = w_up.shape[1]
    tf = MLP_TF
    tm = min(1024, t)
    return pl.pallas_call(
        _mlp_kernel,
        grid=(t // tm, dff // tf),
        in_specs=[
            pl.BlockSpec((tm, d), lambda i, f: (i, 0)),
            pl.BlockSpec((d, tf), lambda i, f: (0, f)),
            pl.BlockSpec((tf, d), lambda i, f: (f, 0)),
            pl.BlockSpec((1, d), lambda i, f: (0, 0)),
        ],
        out_specs=pl.BlockSpec((tm, d), lambda i, f: (i, 0)),
        out_shape=jax.ShapeDtypeStruct((t, d), F32),
        compiler_params=_params(("parallel", "arbitrary"), 58),
        name="mlp",
    )(h2, w_up, w_down, g_post)


def _ple_kernel(x1_ref, r_ref, p_ref, wple_ref, wgate_ref, g_ref, o_ref):
    x = x1_ref[...] + r_ref[...]
    e = _rms(jnp.dot(p_ref[...].astype(BF16), wple_ref[...], preferred_element_type=F32), g_ref[...])
    gate = _sigmoid(jnp.dot(x.astype(BF16), wgate_ref[...], preferred_element_type=F32))
    o_ref[...] = x + gate * e


def _ple(x1, r, p2d, w_ple, w_gate, g_ple):
    t, d = x1.shape
    tm = min(512, t)
    const = lambda i: (0, 0)
    single = pl.Buffered(1)
    return pl.pallas_call(
        _ple_kernel,
        grid=(t // tm,),
        in_specs=[
            pl.BlockSpec((tm, d), lambda i: (i, 0)),
            pl.BlockSpec((tm, d), lambda i: (i, 0)),
            pl.BlockSpec((tm, p2d.shape[1]), lambda i: (i, 0)),
            pl.BlockSpec(w_ple.shape, const, pipeline_mode=single),
            pl.BlockSpec(w_gate.shape, const, pipeline_mode=single),
            pl.BlockSpec((1, d), const),
        ],
        out_specs=pl.BlockSpec((tm, d), lambda i: (i, 0)),
        out_shape=jax.ShapeDtypeStruct((t, d), F32),
        compiler_params=_params(("parallel",), 48),
        name="ple",
    )(x1, r, p2d, w_ple, w_gate, g_ple)


def _reorder_w_in(w):
    o_kr = 3 * SB_WIDTH + MLA_Q_RANK + MLA_KV_RANK
    o_gate = o_kr + MLA_ROPE_DIM
    k_rope = w[:, o_kr:o_gate]
    col_scale = jnp.where(jnp.arange(o_kr) < SB_WIDTH, SB_HEAD_DIM ** -0.5 * LOG2E, 1.0).astype(F32)
    attn = (w[:, :o_kr] * col_scale[None, :]).astype(BF16)
    gates = w[:, o_gate:].astype(BF16)
    return (jnp.concatenate([attn, gates], axis=1),
            jnp.concatenate([k_rope, k_rope], axis=1).astype(BF16))


def _reorder_w_q_up(w):
    w3 = w.reshape(MLA_Q_RANK, MLA_HEADS, MLA_QK_DIM)
    nope = w3[:, :, :MLA_NOPE_DIM].reshape(MLA_Q_RANK, MLA_HEADS * MLA_NOPE_DIM)
    rope = w3[:, :, MLA_NOPE_DIM:]
    rope = jnp.concatenate([rope, rope], axis=-1).reshape(MLA_Q_RANK, MLA_HEADS * LANES)
    return jnp.concatenate([nope, rope], axis=1).astype(BF16)


def _reorder_w_kv_up(w):
    w3 = w.reshape(MLA_KV_RANK, MLA_HEADS, MLA_NOPE_DIM + MLA_V_DIM)
    k_nope = w3[:, :, :MLA_NOPE_DIM].reshape(MLA_KV_RANK, MLA_HEADS * MLA_NOPE_DIM)
    v = w3[:, :, MLA_NOPE_DIM:].reshape(MLA_KV_RANK, MLA_HEADS * MLA_V_DIM)
    return jnp.concatenate([k_nope, v], axis=1).astype(BF16)


def _cumsum_matrix(blk):
    j = jnp.arange(blk)[:, None]
    s = jnp.arange(blk)[None, :]
    return (j > s).astype(BF16)


def _layer(x2d, p2d, pos, invf, batch, seq, g_pre_mix, w_in, g_cq, g_ckv, w_q_up, w_kv_up,
           w_sb_o, w_mla_o, w_out, g_post_mix, g_pre_mlp, w_up, w_down, g_post_mlp,
           w_ple, g_ple, w_ple_gate):
    row = lambda g: g.reshape(1, -1).astype(F32)
    proj, kr = _in_proj(x2d, row(g_pre_mix), *_reorder_w_in(w_in))
    o_sb = _sb_attn(proj, _cumsum_matrix(min(256, seq)), batch, seq)
    q, k, v = _mla_prep(proj, kr, pos, invf, row(g_cq), row(g_ckv),
                        _reorder_w_q_up(w_q_up), _reorder_w_kv_up(w_kv_up))
    o_mla = _mla_attn(q, k, v, batch, seq)
    x1, h2 = _mix(x2d, o_sb, o_mla, proj, w_sb_o.astype(BF16), w_mla_o.astype(BF16),
                  w_out.astype(BF16), row(g_post_mix), row(g_pre_mlp))
    r = _mlp(h2, w_up.astype(BF16), w_down.astype(BF16), row(g_post_mlp))
    return _ple(x1, r, p2d, w_ple.astype(BF16), w_ple_gate.astype(BF16), row(g_ple))


def kernel(x, p, positions, g_pre_mix, w_in, g_cq, g_ckv, w_q_up, w_kv_up, w_sb_o, w_mla_o, w_out, g_post_mix, g_pre_mlp, w_up, w_down, g_post_mlp, w_ple, g_ple, w_ple_gate):
    batch, seq, d = x.shape
    assert d == D_MODEL and seq % 256 == 0, "sequence must tile into stick-breaking blocks"
    t = batch * seq
    half = MLA_ROPE_DIM // 2
    inv_freq = ROPE_THETA ** (-jnp.arange(half, dtype=F32) / half)
    invf = jnp.tile(inv_freq, LANES // half).reshape(1, LANES)
    pos = positions.reshape(t, 1).astype(jnp.int32)
    x2d = x.reshape(t, d)
    for i in range(p.shape[0]):
        x2d = _layer(x2d, p[i].reshape(t, PLE_DIM), pos, invf, batch, seq, g_pre_mix[i], w_in[i],
                     g_cq[i], g_ckv[i], w_q_up[i], w_kv_up[i], w_sb_o[i], w_mla_o[i], w_out[i],
                     g_post_mix[i], g_pre_mlp[i], w_up[i], w_down[i], g_post_mlp[i],
                     w_ple[i], g_ple[i], w_ple_gate[i])
    return x2d.reshape(batch, seq, d)
```

```python
import functools

import jax
import jax.numpy as jnp
from jax import lax
from jax.experimental import pallas as pl
from jax.experimental.pallas import tpu as pltpu

F32 = jnp.float32
BF16 = jnp.bfloat16

D_MODEL = 2048
PLE_DIM = 256
SB_HEADS = 8
SB_HEAD_DIM = 128
MLA_HEADS = 8
MLA_NOPE_DIM = 128
MLA_ROPE_DIM = 64
MLA_V_DIM = 128
MLA_Q_RANK = 512
MLA_KV_RANK = 512
D_FF = 4 * D_MODEL
ROPE_THETA = 10000.0
EPS = 1e-6
SB_WIDTH = SB_HEADS * SB_HEAD_DIM
MLA_WIDTH = MLA_HEADS * MLA_V_DIM
MLA_QK_DIM = MLA_NOPE_DIM + MLA_ROPE_DIM
MLA_QK_PAD = 256

LANES = 128
MIB = 1024 * 1024

BLK_SB_Q = 0
BLK_SB_K = 8
BLK_SB_V = 16
BLK_CQ = 24
BLK_CKV = 28
BLK_GATE_SB = 32
BLK_GATE_MLA = 48

IN_TILE_BLOCKS = 16
ROWS_IN_PROJ = 1024
ROWS_MLA_PREP = 512
ROWS_MIX = 512
ROWS_MLP = 1024
MLP_TF = 1024
ROWS_PLE = 512
SB_BLK = 256
SB_HEADS_PER_STEP = 4
MLA_BLK = 512
MLA_ROW_BLOCKS = 4
MLA_HEADS_PER_STEP = 2
VMEM_MIB = dict(in_proj=58, sb_attn=48, mla_prep=48, mla_attn=52, mix=56, mlp=58, ple=48)

LOG2E = 1.4426950408889634

SB_LOG2_UNDERFLOW = -104.0 * LOG2E


def _params(semantics, vmem_mib):
    return pltpu.CompilerParams(dimension_semantics=semantics, vmem_limit_bytes=vmem_mib * MIB)


def _rms(xf, g):
    y = xf * lax.rsqrt(jnp.mean(xf * xf, axis=-1, keepdims=True) + EPS)
    return y * g


def _sigmoid(x):
    return 1.0 / (1.0 + jnp.exp(-x))


def _cat_blocks(ref, n):
    return jnp.concatenate([ref[c] for c in range(n)], axis=1)


def _in_proj_kernel(x_ref, g_ref, w_ref, wkr_ref, o_ref, kr_ref, h_ref, *, nblk):
    @pl.when(pl.program_id(1) == 0)
    def _():
        h = _rms(x_ref[...], g_ref[...]).astype(BF16)
        h_ref[...] = h
        kr_ref[...] = jnp.dot(h, wkr_ref[...], preferred_element_type=F32).astype(BF16)

    acc = jnp.dot(h_ref[...], w_ref[...], preferred_element_type=F32)
    for c in range(nblk):
        o_ref[c] = acc[:, c * LANES:(c + 1) * LANES].astype(BF16)


def _in_proj(x2d, g, w, w_kr):
    t, d = x2d.shape
    n = w.shape[1]
    tn = IN_TILE_BLOCKS * LANES
    tm = min(ROWS_IN_PROJ, t)
    return pl.pallas_call(
        functools.partial(_in_proj_kernel, nblk=IN_TILE_BLOCKS),
        grid=(t // tm, n // tn),
        in_specs=[
            pl.BlockSpec((tm, d), lambda i, j: (i, 0)),
            pl.BlockSpec((1, d), lambda i, j: (0, 0)),
            pl.BlockSpec((d, tn), lambda i, j: (0, j)),
            pl.BlockSpec((d, LANES), lambda i, j: (0, 0)),
        ],
        out_specs=[
            pl.BlockSpec((IN_TILE_BLOCKS, tm, LANES), lambda i, j: (j, i, 0)),
            pl.BlockSpec((tm, LANES), lambda i, j: (i, 0)),
        ],
        out_shape=[
            jax.ShapeDtypeStruct((n // LANES, t, LANES), BF16),
            jax.ShapeDtypeStruct((t, LANES), BF16),
        ],
        scratch_shapes=[pltpu.VMEM((tm, d), BF16)],
        compiler_params=_params(("parallel", "arbitrary"), VMEM_MIB["in_proj"]),
        name="in_proj",
    )(x2d, g, w, w_kr)


def _sb_chunk(q, k, u, mask, blk):
    z = lax.dot_general(q, k, (((1,), (1,)), ((), ())), preferred_element_type=F32)
    sign_bit = jnp.uint32(0x80000000)
    neg_abs = lax.bitcast_convert_type(lax.bitcast_convert_type(z, jnp.uint32) | sign_bit, F32)
    c = jnp.log(1.0 + jnp.exp2(neg_abs)) * LOG2E
    neg_part = jnp.minimum(z, 0.0)
    log_beta = neg_part - c
    log_fail = (neg_abs - neg_part) - c
    if mask is not None:
        log_fail = jnp.where(mask, log_fail, 0.0)
    hi = log_fail.astype(BF16)
    lo = (log_fail - hi.astype(F32)).astype(BF16)
    later = jnp.dot(jnp.concatenate([hi, lo], axis=1), u, preferred_element_type=F32)
    return log_beta + later, jnp.sum(log_fail, axis=1, keepdims=True)


def _sb_kernel(q_ref, k_ref, v_ref, u_ref, o_ref, acc_ref, carry_ref, *, blk, heads):
    qi = pl.program_id(2)
    u = u_ref[...]

    def wide(c):
        return jnp.concatenate([c] * (blk // LANES), axis=1)

    def kv(g, kb):
        start = pl.multiple_of(kb * blk, blk)
        return k_ref[g, pl.ds(start, blk), :], v_ref[g, pl.ds(start, blk), :]

    def pv(w, v):
        return jnp.dot(w.astype(BF16), v, preferred_element_type=F32)

    row = lax.broadcasted_iota(jnp.int32, (blk, blk), 0)
    col = lax.broadcasted_iota(jnp.int32, (blk, blk), 1)
    mask = col < row
    no_prev = jnp.where(qi > 0, 0.0, -jnp.inf)
    remaining = []
    for g in range(heads):
        q = q_ref[g]
        k_a, v_a = kv(g, qi)
        k_b, v_b = kv(g, jnp.maximum(qi - 1, 0))
        logw_a, total_a = _sb_chunk(q, k_a, u, mask, blk)
        logw_b, total_b = _sb_chunk(q, k_b, u, None, blk)
        w_a = jnp.where(mask, jnp.exp2(logw_a), 0.0)
        w_b = jnp.exp2(logw_b + (total_a + no_prev))
        acc_ref[g] = pv(w_a, v_a) + pv(w_b, v_b)
        carry = total_a + total_b
        carry_ref[g] = jnp.broadcast_to(carry, (blk, LANES))
        remaining.append(jnp.max(carry))

    for g in range(heads):
        q = q_ref[g]

        def body(state, g=g, q=q):
            kb, _ = state
            k, v = kv(g, kb)
            logw, total = _sb_chunk(q, k, u, None, blk)
            carry = carry_ref[g]
            acc_ref[g] += pv(jnp.exp2(logw + wide(carry)), v)
            carry = carry + total
            carry_ref[g] = carry
            return kb - 1, jnp.max(carry)

        def cond(state):
            kb, left = state
            return jnp.logical_and(kb >= 0, left > SB_LOG2_UNDERFLOW)

        lax.while_loop(cond, body, (qi - 2, remaining[g]))
        o_ref[g] = acc_ref[g].astype(BF16)


def _sb_attn(proj, cum_mat, batch, seq):
    blk = min(SB_BLK, seq)
    heads = SB_HEADS_PER_STEP
    nq = seq // blk
    t = batch * seq
    return pl.pallas_call(
        functools.partial(_sb_kernel, blk=blk, heads=heads),
        grid=(batch, SB_HEADS // heads, nq),
        in_specs=[
            pl.BlockSpec((heads, blk, LANES), lambda b, h, i: (BLK_SB_Q // heads + h, b * nq + i, 0)),
            pl.BlockSpec((heads, seq, LANES), lambda b, h, i: (BLK_SB_K // heads + h, b, 0)),
            pl.BlockSpec((heads, seq, LANES), lambda b, h, i: (BLK_SB_V // heads + h, b, 0)),
            pl.BlockSpec((2 * blk, blk), lambda b, h, i: (0, 0)),
        ],
        out_specs=pl.BlockSpec((heads, blk, LANES), lambda b, h, i: (h, b * nq + i, 0)),
        out_shape=jax.ShapeDtypeStruct((SB_HEADS, t, LANES), BF16),
        scratch_shapes=[pltpu.VMEM((heads, blk, LANES), F32), pltpu.VMEM((heads, blk, LANES), F32)],
        compiler_params=_params(("parallel", "parallel", "arbitrary"), VMEM_MIB["sb_attn"]),
        name="sb_attn",
    )(proj, proj, proj, cum_mat)


def _mla_prep_kernel(cq_ref, ckv_ref, kr_ref, pos_ref, invf_ref, gq_ref, gkv_ref, wq_ref, wkv_ref,
                     q_out, k_out, v_out):
    nlat = MLA_Q_RANK // LANES
    cq = _cat_blocks(cq_ref, nlat).astype(F32)
    ckv = _cat_blocks(ckv_ref, nlat).astype(F32)
    q = jnp.dot(_rms(cq, gq_ref[...]).astype(BF16), wq_ref[...], preferred_element_type=F32)
    kv = jnp.dot(_rms(ckv, gkv_ref[...]).astype(BF16), wkv_ref[...], preferred_element_type=F32)

    half = MLA_ROPE_DIM // 2
    groups = LANES // half
    rows = pos_ref.shape[0] // groups
    lane = lax.broadcasted_iota(jnp.int32, (rows, LANES), 1)
    pos = pos_ref[...].astype(F32)
    packed = pos[(groups - 1) * rows:]
    for j in range(groups - 2, -1, -1):
        packed = jnp.where(lane < (j + 1) * half, pos[j * rows:(j + 1) * rows], packed)
    ang = packed * invf_ref[...]
    cos_p, sin_p = jnp.cos(ang), jnp.sin(ang)

    def unpack(t, j, k):
        shift = ((k - j) * half) % LANES
        return pltpu.roll(t, shift, axis=1) if shift else t

    cos_t, sin_t = [], []
    for j in range(groups):
        cos_t.append(jnp.where(lane < half, unpack(cos_p, j, 0),
                               jnp.where(lane < MLA_ROPE_DIM, unpack(cos_p, j, 1), 0.0)))
        sin_t.append(jnp.where(lane < half, -unpack(sin_p, j, 0),
                               jnp.where(lane < MLA_ROPE_DIM, unpack(sin_p, j, 1), 0.0)))
    cos_t = jnp.concatenate(cos_t, axis=0)
    sin_t = jnp.concatenate(sin_t, axis=0)

    def rope(a):
        return a * cos_t + pltpu.roll(a, half, axis=1) * sin_t

    k_rope = rope(kr_ref[...].astype(F32))
    scale = MLA_QK_DIM ** -0.5 * LOG2E
    for h in range(MLA_HEADS):
        lo, hi = h * LANES, (h + 1) * LANES
        q_nope = q[:, lo:hi] * scale
        q_rope = rope(q[:, MLA_HEADS * LANES + lo:MLA_HEADS * LANES + hi]) * scale
        q_out[h] = jnp.concatenate([q_nope, q_rope], axis=1).astype(BF16)
        k_out[h] = jnp.concatenate([kv[:, lo:hi], k_rope], axis=1).astype(BF16)
        v_out[h] = kv[:, MLA_HEADS * LANES + lo:MLA_HEADS * LANES + hi].astype(BF16)


def _mla_prep(proj, kr, pos, invf, g_cq, g_ckv, wq, wkv):
    t = proj.shape[1]
    tm = min(ROWS_MLA_PREP, t)
    nlat = MLA_Q_RANK // LANES
    const = lambda i: (0, 0)
    return pl.pallas_call(
        _mla_prep_kernel,
        grid=(t // tm,),
        in_specs=[
            pl.BlockSpec((nlat, tm, LANES), lambda i: (BLK_CQ // nlat, i, 0)),
            pl.BlockSpec((nlat, tm, LANES), lambda i: (BLK_CKV // nlat, i, 0)),
            pl.BlockSpec((tm, LANES), lambda i: (i, 0)),
            pl.BlockSpec((tm, 1), lambda i: (i, 0)),
            pl.BlockSpec((1, LANES), const),
            pl.BlockSpec((1, MLA_Q_RANK), const),
            pl.BlockSpec((1, MLA_KV_RANK), const),
            pl.BlockSpec(wq.shape, const),
            pl.BlockSpec(wkv.shape, const),
        ],
        out_specs=[
            pl.BlockSpec((MLA_HEADS, tm, MLA_QK_PAD), lambda i: (0, i, 0)),
            pl.BlockSpec((MLA_HEADS, tm, MLA_QK_PAD), lambda i: (0, i, 0)),
            pl.BlockSpec((MLA_HEADS, tm, LANES), lambda i: (0, i, 0)),
        ],
        out_shape=[
            jax.ShapeDtypeStruct((MLA_HEADS, t, MLA_QK_PAD), BF16),
            jax.ShapeDtypeStruct((MLA_HEADS, t, MLA_QK_PAD), BF16),
            jax.ShapeDtypeStruct((MLA_HEADS, t, LANES), BF16),
        ],
        compiler_params=_params(("parallel",), VMEM_MIB["mla_prep"]),
        name="mla_prep",
    )(proj, proj, kr, pos, invf, g_cq, g_ckv, wq, wkv)


def _mla_kernel(q_ref, k_ref, v_ref, o_ref, m_ref, acc_ref, *, blk, heads, subs):
    qi = pl.program_id(2)
    m_ref[...] = jnp.full_like(m_ref, -jnp.inf)
    acc_ref[...] = jnp.zeros_like(acc_ref)

    def step(kb, kinds):
        start = pl.multiple_of(kb * blk, blk)
        for g in range(heads):
            k = k_ref[g, pl.ds(start, blk), :]
            v = v_ref[g, pl.ds(start, blk), :]
            v_ext = jnp.concatenate([v, jnp.ones_like(v)], axis=1)
            for r, kind in enumerate(kinds):
                if kind is None:
                    continue
                rows = slice(r * blk, (r + 1) * blk)
                s = lax.dot_general(q_ref[g, rows, :], k, (((1,), (1,)), ((), ())),
                                    preferred_element_type=F32)
                if kind == "diag":
                    row = lax.broadcasted_iota(jnp.int32, (blk, blk), 0)
                    col = lax.broadcasted_iota(jnp.int32, (blk, blk), 1)
                    s = jnp.where(col <= row, s, -jnp.inf)
                m_prev = m_ref[g, rows, :]
                m_new = jnp.maximum(m_prev, jnp.max(s, axis=1, keepdims=True))
                alpha = jnp.exp2(m_prev - m_new)
                p = jnp.exp2(s - jnp.concatenate([m_new] * (blk // LANES), axis=1))
                acc_ref[g, rows, :] = (jnp.concatenate([alpha, alpha], axis=1) * acc_ref[g, rows, :]
                                       + jnp.dot(p.astype(BF16), v_ext, preferred_element_type=F32))
                m_ref[g, rows, :] = m_new

    def below_diagonal(j, c):
        for d in range(subs):
            step(subs * j + d, ("full",) * subs)
        return c

    lax.fori_loop(0, qi, below_diagonal, 0)
    for d in range(subs):
        step(subs * qi + d, tuple("diag" if r == d else ("full" if r > d else None)
                                  for r in range(subs)))
    for g in range(heads):
        acc = acc_ref[g]
        o_ref[g] = (acc[:, :LANES] / acc[:, LANES:]).astype(BF16)


def _mla_attn(q, k, v, batch, seq):
    blk = min(MLA_BLK, seq)
    subs = min(MLA_ROW_BLOCKS, seq // blk)
    heads = MLA_HEADS_PER_STEP
    tq = subs * blk
    assert seq % tq == 0, "sequence must tile into MLA query tiles"
    nq = seq // tq
    t = batch * seq
    return pl.pallas_call(
        functools.partial(_mla_kernel, blk=blk, heads=heads, subs=subs),
        grid=(batch, MLA_HEADS // heads, nq),
        in_specs=[
            pl.BlockSpec((heads, tq, MLA_QK_PAD), lambda b, h, i: (h, b * nq + i, 0)),
            pl.BlockSpec((heads, seq, MLA_QK_PAD), lambda b, h, i: (h, b, 0)),
            pl.BlockSpec((heads, seq, LANES), lambda b, h, i: (h, b, 0)),
        ],
        out_specs=pl.BlockSpec((heads, tq, LANES), lambda b, h, i: (h, b * nq + i, 0)),
        out_shape=jax.ShapeDtypeStruct((MLA_HEADS, t, LANES), BF16),
        scratch_shapes=[pltpu.VMEM((heads, tq, LANES), F32),
                        pltpu.VMEM((heads, tq, 2 * LANES), F32)],
        compiler_params=_params(("parallel", "parallel", "arbitrary"), VMEM_MIB["mla_attn"]),
        name="mla_attn",
    )(q, k, v)


def _mix_kernel(x_ref, osb_ref, omla_ref, gsb_ref, gmla_ref, wsb_ref, wmla_ref, wout_ref,
                gpost_ref, gpre_ref, x1_ref, h2_ref):
    nd = D_MODEL // LANES
    tm = x_ref.shape[0]
    half = tm // 2
    for rows in (slice(0, half), slice(half, tm)):
        cat = lambda ref, n: jnp.concatenate([ref[c, rows, :] for c in range(n)], axis=1)
        a = jnp.dot(cat(osb_ref, SB_HEADS), wsb_ref[...], preferred_element_type=F32)
        b = jnp.dot(cat(omla_ref, MLA_HEADS), wmla_ref[...], preferred_element_type=F32)
        gate_sb = cat(gsb_ref, nd).astype(F32)
        gate_mla = cat(gmla_ref, nd).astype(F32)
        mixed = _sigmoid(gate_sb) * a + _sigmoid(gate_mla) * b
        y = jnp.dot(mixed.astype(BF16), wout_ref[...], preferred_element_type=F32)
        x1 = x_ref[rows, :] + _rms(y, gpost_ref[...])
        x1_ref[rows, :] = x1
        h2_ref[rows, :] = _rms(x1, gpre_ref[...]).astype(BF16)


def _mix(x2d, o_sb, o_mla, proj, w_sb_o, w_mla_o, w_out, g_post, g_pre):
    t, d = x2d.shape
    tm = min(ROWS_MIX, t)
    nd = d // LANES
    const = lambda i: (0, 0)
    single = pl.Buffered(1)
    return pl.pallas_call(
        _mix_kernel,
        grid=(t // tm,),
        in_specs=[
            pl.BlockSpec((tm, d), lambda i: (i, 0)),
            pl.BlockSpec((SB_HEADS, tm, LANES), lambda i: (0, i, 0)),
            pl.BlockSpec((MLA_HEADS, tm, LANES), lambda i: (0, i, 0)),
            pl.BlockSpec((nd, tm, LANES), lambda i: (BLK_GATE_SB // nd, i, 0)),
            pl.BlockSpec((nd, tm, LANES), lambda i: (BLK_GATE_MLA // nd, i, 0)),
            pl.BlockSpec(w_sb_o.shape, const, pipeline_mode=single),
            pl.BlockSpec(w_mla_o.shape, const, pipeline_mode=single),
            pl.BlockSpec(w_out.shape, const, pipeline_mode=single),
            pl.BlockSpec((1, d), const),
            pl.BlockSpec((1, d), const),
        ],
        out_specs=[
            pl.BlockSpec((tm, d), lambda i: (i, 0)),
            pl.BlockSpec((tm, d), lambda i: (i, 0)),
        ],
        out_shape=[
            jax.ShapeDtypeStruct((t, d), F32),
            jax.ShapeDtypeStruct((t, d), BF16),
        ],
        compiler_params=_params(("parallel",), VMEM_MIB["mix"]),
        name="mix",
    )(x2d, o_sb, o_mla, proj, proj, w_sb_o, w_mla_o, w_out, g_post, g_pre)


def _mlp_kernel(h_ref, wup_ref, wdown_ref, g_ref, o_ref):
    f = pl.program_id(1)

    @pl.when(f == 0)
    def _():
        o_ref[...] = jnp.zeros_like(o_ref)

    u = jnp.dot(h_ref[...], wup_ref[...], preferred_element_type=F32)
    u = jnp.square(jnp.maximum(u, 0.0))
    o_ref[...] += jnp.dot(u.astype(BF16), wdown_ref[...], preferred_element_type=F32)

    @pl.when(f == pl.num_programs(1) - 1)
    def _():
        o_ref[...] = _rms(o_ref[...], g_ref[...])


def _mlp(h2, w_up, w_down, g_post):
    t, d = h2.shape
    dff = w_up.shape[1]
    tf = MLP_TF
    tm = min(ROWS_MLP, t)
    return pl.pallas_call(
        _mlp_kernel,
        grid=(t // tm, dff // tf),
        in_specs=[
            pl.BlockSpec((tm, d), lambda i, f: (i, 0)),
            pl.BlockSpec((d, tf), lambda i, f: (0, f)),
            pl.BlockSpec((tf, d), lambda i, f: (f, 0)),
            pl.BlockSpec((1, d), lambda i, f: (0, 0)),
        ],
        out_specs=pl.BlockSpec((tm, d), lambda i, f: (i, 0)),
        out_shape=jax.ShapeDtypeStruct((t, d), F32),
        compiler_params=_params(("parallel", "arbitrary"), VMEM_MIB["mlp"]),
        name="mlp",
    )(h2, w_up, w_down, g_post)


def _ple_kernel(x1_ref, r_ref, p_ref, wple_ref, wgate_ref, g_ref, o_ref):
    x = x1_ref[...] + r_ref[...]
    e = _rms(jnp.dot(p_ref[...].astype(BF16), wple_ref[...], preferred_element_type=F32), g_ref[...])
    gate = _sigmoid(jnp.dot(x.astype(BF16), wgate_ref[...], preferred_element_type=F32))
    o_ref[...] = x + gate * e


def _ple(x1, r, p2d, w_ple, w_gate, g_ple):
    t, d = x1.shape
    tm = min(ROWS_PLE, t)
    const = lambda i: (0, 0)
    single = pl.Buffered(1)
    return pl.pallas_call(
        _ple_kernel,
        grid=(t // tm,),
        in_specs=[
            pl.BlockSpec((tm, d), lambda i: (i, 0)),
            pl.BlockSpec((tm, d), lambda i: (i, 0)),
            pl.BlockSpec((tm, p2d.shape[1]), lambda i: (i, 0)),
            pl.BlockSpec(w_ple.shape, const, pipeline_mode=single),
            pl.BlockSpec(w_gate.shape, const, pipeline_mode=single),
            pl.BlockSpec((1, d), const),
        ],
        out_specs=pl.BlockSpec((tm, d), lambda i: (i, 0)),
        out_shape=jax.ShapeDtypeStruct((t, d), F32),
        compiler_params=_params(("parallel",), VMEM_MIB["ple"]),
        name="ple",
    )(x1, r, p2d, w_ple, w_gate, g_ple)


def _reorder_w_in(w):
    o_kr = 3 * SB_WIDTH + MLA_Q_RANK + MLA_KV_RANK
    o_gate = o_kr + MLA_ROPE_DIM
    k_rope = w[:, o_kr:o_gate]
    col_scale = jnp.where(jnp.arange(o_kr) < SB_WIDTH, SB_HEAD_DIM ** -0.5 * LOG2E, 1.0).astype(F32)
    attn = (w[:, :o_kr] * col_scale[None, :]).astype(BF16)
    gates = w[:, o_gate:].astype(BF16)
    return (jnp.concatenate([attn, gates], axis=1),
            jnp.concatenate([k_rope, k_rope], axis=1).astype(BF16))


def _reorder_w_q_up(w):
    w3 = w.reshape(MLA_Q_RANK, MLA_HEADS, MLA_QK_DIM)
    nope = w3[:, :, :MLA_NOPE_DIM].reshape(MLA_Q_RANK, MLA_HEADS * MLA_NOPE_DIM)
    rope = w3[:, :, MLA_NOPE_DIM:]
    rope = jnp.concatenate([rope, rope], axis=-1).reshape(MLA_Q_RANK, MLA_HEADS * LANES)
    return jnp.concatenate([nope, rope], axis=1).astype(BF16)


def _reorder_w_kv_up(w):
    w3 = w.reshape(MLA_KV_RANK, MLA_HEADS, MLA_NOPE_DIM + MLA_V_DIM)
    k_nope = w3[:, :, :MLA_NOPE_DIM].reshape(MLA_KV_RANK, MLA_HEADS * MLA_NOPE_DIM)
    v = w3[:, :, MLA_NOPE_DIM:].reshape(MLA_KV_RANK, MLA_HEADS * MLA_V_DIM)
    return jnp.concatenate([k_nope, v], axis=1).astype(BF16)


def _cumsum_matrix(blk):
    j = jnp.arange(blk)[:, None]
    s = jnp.arange(blk)[None, :]
    tri = (j > s).astype(BF16)
    return jnp.concatenate([tri, tri], axis=0)


def _layer(x2d, p2d, pos, invf, batch, seq, g_pre_mix, w_in, g_cq, g_ckv, w_q_up, w_kv_up,
           w_sb_o, w_mla_o, w_out, g_post_mix, g_pre_mlp, w_up, w_down, g_post_mlp,
           w_ple, g_ple, w_ple_gate):
    row = lambda g: g.reshape(1, -1).astype(F32)
    proj, kr = _in_proj(x2d, row(g_pre_mix), *_reorder_w_in(w_in))
    o_sb = _sb_attn(proj, _cumsum_matrix(min(SB_BLK, seq)), batch, seq)
    q, k, v = _mla_prep(proj, kr, pos, invf, row(g_cq), row(g_ckv),
                        _reorder_w_q_up(w_q_up), _reorder_w_kv_up(w_kv_up))
    o_mla = _mla_attn(q, k, v, batch, seq)
    x1, h2 = _mix(x2d, o_sb, o_mla, proj, w_sb_o.astype(BF16), w_mla_o.astype(BF16),
                  w_out.astype(BF16), row(g_post_mix), row(g_pre_mlp))
    r = _mlp(h2, w_up.astype(BF16), w_down.astype(BF16), row(g_post_mlp))
    return _ple(x1, r, p2d, w_ple.astype(BF16), w_ple_gate.astype(BF16), row(g_ple))


def kernel(x, p, positions, g_pre_mix, w_in, g_cq, g_ckv, w_q_up, w_kv_up, w_sb_o, w_mla_o, w_out, g_post_mix, g_pre_mlp, w_up, w_down, g_post_mlp, w_ple, g_ple, w_ple_gate):
    batch, seq, d = x.shape
    assert d == D_MODEL and seq % 256 == 0, "sequence must tile into stick-breaking blocks"
    t = batch * seq
    half = MLA_ROPE_DIM // 2
    inv_freq = ROPE_THETA ** (-jnp.arange(half, dtype=F32) / half)
    invf = jnp.tile(inv_freq, LANES // half).reshape(1, LANES)
    pos = positions.reshape(t, 1).astype(jnp.int32)
    x2d = x.reshape(t, d)
    for i in range(p.shape[0]):
        x2d = _layer(x2d, p[i].reshape(t, PLE_DIM), pos, invf, batch, seq, g_pre_mix[i], w_in[i],
                     g_cq[i], g_ckv[i], w_q_up[i], w_kv_up[i], w_sb_o[i], w_mla_o[i], w_out[i],
                     g_post_mix[i], g_pre_mlp[i], w_up[i], w_down[i], g_post_mlp[i],
                     w_ple[i], g_ple[i], w_ple_gate[i])
    return x2d.reshape(batch, seq, d)
```

```python
import functools

import jax
import jax.numpy as jnp
from jax import lax
from jax.experimental import pallas as pl
from jax.experimental.pallas import tpu as pltpu

F32 = jnp.float32
BF16 = jnp.bfloat16

D_MODEL = 2048
PLE_DIM = 256
SB_HEADS = 8
SB_HEAD_DIM = 128
MLA_HEADS = 8
MLA_NOPE_DIM = 128
MLA_ROPE_DIM = 64
MLA_V_DIM = 128
MLA_Q_RANK = 512
MLA_KV_RANK = 512
D_FF = 4 * D_MODEL
ROPE_THETA = 10000.0
EPS = 1e-6
SB_WIDTH = SB_HEADS * SB_HEAD_DIM
MLA_WIDTH = MLA_HEADS * MLA_V_DIM
MLA_QK_DIM = MLA_NOPE_DIM + MLA_ROPE_DIM
MLA_QK_PAD = 256

LANES = 128
BF16_SUBLANES = 16
MIB = 1024 * 1024

BLK_SB_Q = 0
BLK_SB_K = 8
BLK_SB_V = 16
BLK_CQ = 24
BLK_CKV = 28
BLK_GATE_SB = 32
BLK_GATE_MLA = 48

IN_TILE_BLOCKS = 16
ROWS_IN_PROJ = 1024
ROWS_MLA_PREP = 512
ROWS_MIX = 512
ROWS_MLP = 1024
MLP_TF = 1024
ROWS_PLE = 512
SB_BLK = 256
SB_HEADS_PER_STEP = 4
MLA_BLK = 512
MLA_ROW_BLOCKS = 4
MLA_HEADS_PER_STEP = 2
VMEM_MIB = dict(in_proj=58, sb_attn=48, mla_prep=48, mla_attn=56, mix=56, mlp=58, ple=48)

LOG2E = 1.4426950408889634

SB_LOG2_UNDERFLOW = -104.0 * LOG2E


def _params(semantics, vmem_mib):
    return pltpu.CompilerParams(dimension_semantics=semantics, vmem_limit_bytes=vmem_mib * MIB)


def _rms(xf, g):
    y = xf * lax.rsqrt(jnp.mean(xf * xf, axis=-1, keepdims=True) + EPS)
    return y * g


def _sigmoid(x):
    return 1.0 / (1.0 + jnp.exp(-x))


def _cat_blocks(ref, n):
    return jnp.concatenate([ref[c] for c in range(n)], axis=1)


def _in_proj_kernel(x_ref, g_ref, w_ref, wkr_ref, o_ref, kr_ref, h_ref, *, nblk):
    @pl.when(pl.program_id(1) == 0)
    def _():
        h = _rms(x_ref[...], g_ref[...]).astype(BF16)
        h_ref[...] = h
        kr_ref[...] = jnp.dot(h, wkr_ref[...], preferred_element_type=F32).astype(BF16)

    acc = jnp.dot(h_ref[...], w_ref[...], preferred_element_type=F32)
    for c in range(nblk):
        o_ref[c] = acc[:, c * LANES:(c + 1) * LANES].astype(BF16)


def _in_proj(x2d, g, w, w_kr):
    t, d = x2d.shape
    n = w.shape[1]
    tn = IN_TILE_BLOCKS * LANES
    tm = min(ROWS_IN_PROJ, t)
    return pl.pallas_call(
        functools.partial(_in_proj_kernel, nblk=IN_TILE_BLOCKS),
        grid=(t // tm, n // tn),
        in_specs=[
            pl.BlockSpec((tm, d), lambda i, j: (i, 0)),
            pl.BlockSpec((1, d), lambda i, j: (0, 0)),
            pl.BlockSpec((d, tn), lambda i, j: (0, j)),
            pl.BlockSpec((d, LANES), lambda i, j: (0, 0)),
        ],
        out_specs=[
            pl.BlockSpec((IN_TILE_BLOCKS, tm, LANES), lambda i, j: (j, i, 0)),
            pl.BlockSpec((tm, LANES), lambda i, j: (i, 0)),
        ],
        out_shape=[
            jax.ShapeDtypeStruct((n // LANES, t, LANES), BF16),
            jax.ShapeDtypeStruct((t, LANES), BF16),
        ],
        scratch_shapes=[pltpu.VMEM((tm, d), BF16)],
        compiler_params=_params(("parallel", "arbitrary"), VMEM_MIB["in_proj"]),
        name="in_proj",
    )(x2d, g, w, w_kr)


def _sb_chunk(q, k, u, mask, blk):
    z = lax.dot_general(q, k, (((1,), (1,)), ((), ())), preferred_element_type=F32)
    sign_bit = jnp.uint32(0x80000000)
    neg_abs = lax.bitcast_convert_type(lax.bitcast_convert_type(z, jnp.uint32) | sign_bit, F32)
    c = jnp.log(1.0 + jnp.exp2(neg_abs)) * LOG2E
    neg_part = jnp.minimum(z, 0.0)
    log_beta = neg_part - c
    log_fail = (neg_abs - neg_part) - c
    if mask is not None:
        log_fail = jnp.where(mask, log_fail, 0.0)
    hi = log_fail.astype(BF16)
    lo = (log_fail - hi.astype(F32)).astype(BF16)
    later = jnp.dot(jnp.concatenate([hi, lo], axis=1), u, preferred_element_type=F32)
    return log_beta + later, jnp.sum(log_fail, axis=1, keepdims=True)


def _sb_kernel(q_ref, k_ref, v_ref, u_ref, o_ref, acc_ref, carry_ref, *, blk, heads):
    qi = pl.program_id(2)
    u = u_ref[...]

    def wide(c):
        return jnp.concatenate([c] * (blk // LANES), axis=1)

    def kv(g, kb):
        start = pl.multiple_of(kb * blk, blk)
        return k_ref[g, pl.ds(start, blk), :], v_ref[g, pl.ds(start, blk), :]

    def pv(w, v):
        return jnp.dot(w.astype(BF16), v, preferred_element_type=F32)

    row = lax.broadcasted_iota(jnp.int32, (blk, blk), 0)
    col = lax.broadcasted_iota(jnp.int32, (blk, blk), 1)
    mask = col < row
    no_prev = jnp.where(qi > 0, 0.0, -jnp.inf)
    remaining = []
    for g in range(heads):
        q = q_ref[g]
        k_a, v_a = kv(g, qi)
        k_b, v_b = kv(g, jnp.maximum(qi - 1, 0))
        logw_a, total_a = _sb_chunk(q, k_a, u, mask, blk)
        logw_b, total_b = _sb_chunk(q, k_b, u, None, blk)
        w_a = jnp.where(mask, jnp.exp2(logw_a), 0.0)
        w_b = jnp.exp2(logw_b + (total_a + no_prev))
        acc_ref[g] = pv(w_a, v_a) + pv(w_b, v_b)
        carry = total_a + total_b
        carry_ref[g] = jnp.broadcast_to(carry, (blk, LANES))
        remaining.append(jnp.max(carry))

    for g in range(heads):
        q = q_ref[g]

        def body(state, g=g, q=q):
            kb, _ = state
            k, v = kv(g, kb)
            logw, total = _sb_chunk(q, k, u, None, blk)
            carry = carry_ref[g]
            acc_ref[g] += pv(jnp.exp2(logw + wide(carry)), v)
            carry = carry + total
            carry_ref[g] = carry
            return kb - 1, jnp.max(carry)

        def cond(state):
            kb, left = state
            return jnp.logical_and(kb >= 0, left > SB_LOG2_UNDERFLOW)

        lax.while_loop(cond, body, (qi - 2, remaining[g]))
        o_ref[g] = acc_ref[g].astype(BF16)


def _sb_attn(proj, cum_mat, batch, seq):
    blk = min(SB_BLK, seq)
    heads = SB_HEADS_PER_STEP
    nq = seq // blk
    t = batch * seq
    return pl.pallas_call(
        functools.partial(_sb_kernel, blk=blk, heads=heads),
        grid=(batch, SB_HEADS // heads, nq),
        in_specs=[
            pl.BlockSpec((heads, blk, LANES), lambda b, h, i: (BLK_SB_Q // heads + h, b * nq + i, 0)),
            pl.BlockSpec((heads, seq, LANES), lambda b, h, i: (BLK_SB_K // heads + h, b, 0)),
            pl.BlockSpec((heads, seq, LANES), lambda b, h, i: (BLK_SB_V // heads + h, b, 0)),
            pl.BlockSpec((2 * blk, blk), lambda b, h, i: (0, 0)),
        ],
        out_specs=pl.BlockSpec((heads, blk, LANES), lambda b, h, i: (h, b * nq + i, 0)),
        out_shape=jax.ShapeDtypeStruct((SB_HEADS, t, LANES), BF16),
        scratch_shapes=[pltpu.VMEM((heads, blk, LANES), F32), pltpu.VMEM((heads, blk, LANES), F32)],
        compiler_params=_params(("parallel", "parallel", "arbitrary"), VMEM_MIB["sb_attn"]),
        name="sb_attn",
    )(proj, proj, proj, cum_mat)


def _mla_prep_kernel(cq_ref, ckv_ref, kr_ref, pos_ref, invf_ref, gq_ref, gkv_ref, wq_ref, wkv_ref,
                     q_out, k_out, v_out):
    nlat = MLA_Q_RANK // LANES
    cq = _cat_blocks(cq_ref, nlat).astype(F32)
    ckv = _cat_blocks(ckv_ref, nlat).astype(F32)
    q = jnp.dot(_rms(cq, gq_ref[...]).astype(BF16), wq_ref[...], preferred_element_type=F32)
    kv = jnp.dot(_rms(ckv, gkv_ref[...]).astype(BF16), wkv_ref[...], preferred_element_type=F32)

    half = MLA_ROPE_DIM // 2
    groups = LANES // half
    rows = pos_ref.shape[0] // groups
    lane = lax.broadcasted_iota(jnp.int32, (rows, LANES), 1)
    pos = pos_ref[...].astype(F32)
    packed = pos[(groups - 1) * rows:]
    for j in range(groups - 2, -1, -1):
        packed = jnp.where(lane < (j + 1) * half, pos[j * rows:(j + 1) * rows], packed)
    ang = packed * invf_ref[...]
    cos_p, sin_p = jnp.cos(ang), jnp.sin(ang)

    def unpack(t, j, k):
        shift = ((k - j) * half) % LANES
        return pltpu.roll(t, shift, axis=1) if shift else t

    cos_t, sin_t = [], []
    for j in range(groups):
        cos_t.append(jnp.where(lane < half, unpack(cos_p, j, 0),
                               jnp.where(lane < MLA_ROPE_DIM, unpack(cos_p, j, 1), 0.0)))
        sin_t.append(jnp.where(lane < half, -unpack(sin_p, j, 0),
                               jnp.where(lane < MLA_ROPE_DIM, unpack(sin_p, j, 1), 0.0)))
    cos_t = jnp.concatenate(cos_t, axis=0)
    sin_t = jnp.concatenate(sin_t, axis=0)

    def rope(a):
        return a * cos_t + pltpu.roll(a, half, axis=1) * sin_t

    k_rope = rope(kr_ref[...].astype(F32))
    scale = MLA_QK_DIM ** -0.5 * LOG2E
    for h in range(MLA_HEADS):
        lo, hi = h * LANES, (h + 1) * LANES
        q_nope = q[:, lo:hi] * scale
        q_rope = rope(q[:, MLA_HEADS * LANES + lo:MLA_HEADS * LANES + hi]) * scale
        q_out[h] = jnp.concatenate([q_nope, q_rope], axis=1).astype(BF16)
        k_out[h] = jnp.concatenate([kv[:, lo:hi], k_rope], axis=1).astype(BF16)
        v_out[h] = kv[:, MLA_HEADS * LANES + lo:MLA_HEADS * LANES + hi].astype(BF16)


def _mla_prep(proj, kr, pos, invf, g_cq, g_ckv, wq, wkv):
    t = proj.shape[1]
    tm = min(ROWS_MLA_PREP, t)
    nlat = MLA_Q_RANK // LANES
    const = lambda i: (0, 0)
    return pl.pallas_call(
        _mla_prep_kernel,
        grid=(t // tm,),
        in_specs=[
            pl.BlockSpec((nlat, tm, LANES), lambda i: (BLK_CQ // nlat, i, 0)),
            pl.BlockSpec((nlat, tm, LANES), lambda i: (BLK_CKV // nlat, i, 0)),
            pl.BlockSpec((tm, LANES), lambda i: (i, 0)),
            pl.BlockSpec((tm, 1), lambda i: (i, 0)),
            pl.BlockSpec((1, LANES), const),
            pl.BlockSpec((1, MLA_Q_RANK), const),
            pl.BlockSpec((1, MLA_KV_RANK), const),
            pl.BlockSpec(wq.shape, const),
            pl.BlockSpec(wkv.shape, const),
        ],
        out_specs=[
            pl.BlockSpec((MLA_HEADS, tm, MLA_QK_PAD), lambda i: (0, i, 0)),
            pl.BlockSpec((MLA_HEADS, tm, MLA_QK_PAD), lambda i: (0, i, 0)),
            pl.BlockSpec((MLA_HEADS, tm, LANES), lambda i: (0, i, 0)),
        ],
        out_shape=[
            jax.ShapeDtypeStruct((MLA_HEADS, t, MLA_QK_PAD), BF16),
            jax.ShapeDtypeStruct((MLA_HEADS, t, MLA_QK_PAD), BF16),
            jax.ShapeDtypeStruct((MLA_HEADS, t, LANES), BF16),
        ],
        compiler_params=_params(("parallel",), VMEM_MIB["mla_prep"]),
        name="mla_prep",
    )(proj, proj, kr, pos, invf, g_cq, g_ckv, wq, wkv)


def _mla_kernel(*refs, blk, heads, subs, ncast):
    q_ref, k_ref, v_ref = refs[:3]
    o_ref = refs[3 + ncast]
    m_ref, acc_ref = refs[-2:]
    for src, dst in zip(refs[3:3 + ncast], refs[4 + ncast:4 + 2 * ncast]):
        dst[...] = src[...].astype(BF16)
    qi = pl.program_id(2)
    m_ref[...] = jnp.full_like(m_ref, -jnp.inf)
    acc_ref[...] = jnp.zeros_like(acc_ref)

    def step(kb, kinds):
        start = pl.multiple_of(kb * blk, blk)
        for g in range(heads):
            k = k_ref[g, pl.ds(start, blk), :]
            v = v_ref[g, pl.ds(start, blk), :]
            v_ext = jnp.concatenate([v, jnp.ones_like(v)], axis=1)
            for r, kind in enumerate(kinds):
                if kind is None:
                    continue
                rows = slice(r * blk, (r + 1) * blk)
                s = lax.dot_general(q_ref[g, rows, :], k, (((1,), (1,)), ((), ())),
                                    preferred_element_type=F32)
                if kind == "diag":
                    row = lax.broadcasted_iota(jnp.int32, (blk, blk), 0)
                    col = lax.broadcasted_iota(jnp.int32, (blk, blk), 1)
                    s = jnp.where(col <= row, s, -jnp.inf)
                m_prev = m_ref[g, rows, :]
                m_new = jnp.maximum(m_prev, jnp.max(s, axis=1, keepdims=True))
                alpha = jnp.exp2(m_prev - m_new)
                p = jnp.exp2(s - jnp.concatenate([m_new] * (blk // LANES), axis=1))
                acc_ref[g, rows, :] = (jnp.concatenate([alpha, alpha], axis=1) * acc_ref[g, rows, :]
                                       + jnp.dot(p.astype(BF16), v_ext, preferred_element_type=F32))
                m_ref[g, rows, :] = m_new

    def below_diagonal(j, c):
        for d in range(subs):
            step(subs * j + d, ("full",) * subs)
        return c

    lax.fori_loop(0, qi, below_diagonal, 0)
    for d in range(subs):
        step(subs * qi + d, tuple("diag" if r == d else ("full" if r > d else None)
                                  for r in range(subs)))
    for g in range(heads):
        acc = acc_ref[g]
        o_ref[g] = (acc[:, :LANES] / acc[:, LANES:]).astype(BF16)


def _mla_attn(q, k, v, batch, seq, weights):
    blk = min(MLA_BLK, seq)
    subs = min(MLA_ROW_BLOCKS, seq // blk)
    heads = MLA_HEADS_PER_STEP
    tq = subs * blk
    assert seq % tq == 0, "sequence must tile into MLA query tiles"
    nq = seq // tq
    ngroups = MLA_HEADS // heads
    t = batch * seq
    steps = batch * ngroups * nq
    assert all(w.shape[0] % (steps * BF16_SUBLANES) == 0 for w in weights)
    slab = lambda b, h, i: ((b * ngroups + h) * nq + i, 0)
    slabs = [pl.BlockSpec((w.shape[0] // steps, w.shape[1]), slab) for w in weights]
    out = pl.pallas_call(
        functools.partial(_mla_kernel, blk=blk, heads=heads, subs=subs, ncast=len(weights)),
        grid=(batch, ngroups, nq),
        in_specs=[
            pl.BlockSpec((heads, tq, MLA_QK_PAD), lambda b, h, i: (h, b * nq + i, 0)),
            pl.BlockSpec((heads, seq, MLA_QK_PAD), lambda b, h, i: (h, b, 0)),
            pl.BlockSpec((heads, seq, LANES), lambda b, h, i: (h, b, 0)),
        ] + slabs,
        out_specs=[pl.BlockSpec((heads, tq, LANES), lambda b, h, i: (h, b * nq + i, 0))] + slabs,
        out_shape=[jax.ShapeDtypeStruct((MLA_HEADS, t, LANES), BF16)]
        + [jax.ShapeDtypeStruct(w.shape, BF16) for w in weights],
        scratch_shapes=[pltpu.VMEM((heads, tq, LANES), F32),
                        pltpu.VMEM((heads, tq, 2 * LANES), F32)],
        compiler_params=_params(("parallel", "parallel", "arbitrary"), VMEM_MIB["mla_attn"]),
        name="mla_attn",
    )(q, k, v, *weights)
    return out[0], out[1:]


def _mix_kernel(x_ref, osb_ref, omla_ref, gsb_ref, gmla_ref, wsb_ref, wmla_ref, wout_ref,
                gpost_ref, gpre_ref, x1_ref, h2_ref):
    nd = D_MODEL // LANES
    tm = x_ref.shape[0]
    half = tm // 2
    for rows in (slice(0, half), slice(half, tm)):
        cat = lambda ref, n: jnp.concatenate([ref[c, rows, :] for c in range(n)], axis=1)
        a = jnp.dot(cat(osb_ref, SB_HEADS), wsb_ref[...], preferred_element_type=F32)
        b = jnp.dot(cat(omla_ref, MLA_HEADS), wmla_ref[...], preferred_element_type=F32)
        gate_sb = cat(gsb_ref, nd).astype(F32)
        gate_mla = cat(gmla_ref, nd).astype(F32)
        mixed = _sigmoid(gate_sb) * a + _sigmoid(gate_mla) * b
        y = jnp.dot(mixed.astype(BF16), wout_ref[...], preferred_element_type=F32)
        x1 = x_ref[rows, :] + _rms(y, gpost_ref[...])
        x1_ref[rows, :] = x1
        h2_ref[rows, :] = _rms(x1, gpre_ref[...]).astype(BF16)


def _mix(x2d, o_sb, o_mla, proj, w_sb_o, w_mla_o, w_out, g_post, g_pre):
    t, d = x2d.shape
    tm = min(ROWS_MIX, t)
    nd = d // LANES
    const = lambda i: (0, 0)
    single = pl.Buffered(1)
    return pl.pallas_call(
        _mix_kernel,
        grid=(t // tm,),
        in_specs=[
            pl.BlockSpec((tm, d), lambda i: (i, 0)),
            pl.BlockSpec((SB_HEADS, tm, LANES), lambda i: (0, i, 0)),
            pl.BlockSpec((MLA_HEADS, tm, LANES), lambda i: (0, i, 0)),
            pl.BlockSpec((nd, tm, LANES), lambda i: (BLK_GATE_SB // nd, i, 0)),
            pl.BlockSpec((nd, tm, LANES), lambda i: (BLK_GATE_MLA // nd, i, 0)),
            pl.BlockSpec(w_sb_o.shape, const, pipeline_mode=single),
            pl.BlockSpec(w_mla_o.shape, const, pipeline_mode=single),
            pl.BlockSpec(w_out.shape, const, pipeline_mode=single),
            pl.BlockSpec((1, d), const),
            pl.BlockSpec((1, d), const),
        ],
        out_specs=[
            pl.BlockSpec((tm, d), lambda i: (i, 0)),
            pl.BlockSpec((tm, d), lambda i: (i, 0)),
        ],
        out_shape=[
            jax.ShapeDtypeStruct((t, d), F32),
            jax.ShapeDtypeStruct((t, d), BF16),
        ],
        compiler_params=_params(("parallel",), VMEM_MIB["mix"]),
        name="mix",
    )(x2d, o_sb, o_mla, proj, proj, w_sb_o, w_mla_o, w_out, g_post, g_pre)


def _mlp_kernel(h_ref, wup_ref, wdown_ref, g_ref, o_ref):
    f = pl.program_id(1)

    @pl.when(f == 0)
    def _():
        o_ref[...] = jnp.zeros_like(o_ref)

    u = jnp.dot(h_ref[...], wup_ref[...], preferred_element_type=F32)
    u = jnp.square(jnp.maximum(u, 0.0))
    o_ref[...] += jnp.dot(u.astype(BF16), wdown_ref[...], preferred_element_type=F32)

    @pl.when(f == pl.num_programs(1) - 1)
    def _():
        o_ref[...] = _rms(o_ref[...], g_ref[...])


def _mlp(h2, w_up, w_down, g_post):
    t, d = h2.shape
    dff = w_up.shape[1]
    tf = MLP_TF
    tm = min(ROWS_MLP, t)
    return pl.pallas_call(
        _mlp_kernel,
        grid=(t // tm, dff // tf),
        in_specs=[
            pl.BlockSpec((tm, d), lambda i, f: (i, 0)),
            pl.BlockSpec((d, tf), lambda i, f: (0, f)),
            pl.BlockSpec((tf, d), lambda i, f: (f, 0)),
            pl.BlockSpec((1, d), lambda i, f: (0, 0)),
        ],
        out_specs=pl.BlockSpec((tm, d), lambda i, f: (i, 0)),
        out_shape=jax.ShapeDtypeStruct((t, d), F32),
        compiler_params=_params(("parallel", "arbitrary"), VMEM_MIB["mlp"]),
        name="mlp",
    )(h2, w_up, w_down, g_post)


def _ple_kernel(x1_ref, r_ref, p_ref, wple_ref, wgate_ref, g_ref, o_ref):
    x = x1_ref[...] + r_ref[...]
    e = _rms(jnp.dot(p_ref[...].astype(BF16), wple_ref[...], preferred_element_type=F32), g_ref[...])
    gate = _sigmoid(jnp.dot(x.astype(BF16), wgate_ref[...], preferred_element_type=F32))
    o_ref[...] = x + gate * e


def _ple(x1, r, p2d, w_ple, w_gate, g_ple):
    t, d = x1.shape
    tm = min(ROWS_PLE, t)
    const = lambda i: (0, 0)
    single = pl.Buffered(1)
    return pl.pallas_call(
        _ple_kernel,
        grid=(t // tm,),
        in_specs=[
            pl.BlockSpec((tm, d), lambda i: (i, 0)),
            pl.BlockSpec((tm, d), lambda i: (i, 0)),
            pl.BlockSpec((tm, p2d.shape[1]), lambda i: (i, 0)),
            pl.BlockSpec(w_ple.shape, const, pipeline_mode=single),
            pl.BlockSpec(w_gate.shape, const, pipeline_mode=single),
            pl.BlockSpec((1, d), const),
        ],
        out_specs=pl.BlockSpec((tm, d), lambda i: (i, 0)),
        out_shape=jax.ShapeDtypeStruct((t, d), F32),
        compiler_params=_params(("parallel",), VMEM_MIB["ple"]),
        name="ple",
    )(x1, r, p2d, w_ple, w_gate, g_ple)


def _reorder_w_in(w):
    o_kr = 3 * SB_WIDTH + MLA_Q_RANK + MLA_KV_RANK
    o_gate = o_kr + MLA_ROPE_DIM
    k_rope = w[:, o_kr:o_gate]
    col_scale = jnp.where(jnp.arange(o_kr) < SB_WIDTH, SB_HEAD_DIM ** -0.5 * LOG2E, 1.0).astype(F32)
    attn = (w[:, :o_kr] * col_scale[None, :]).astype(BF16)
    gates = w[:, o_gate:].astype(BF16)
    return (jnp.concatenate([attn, gates], axis=1),
            jnp.concatenate([k_rope, k_rope], axis=1).astype(BF16))


def _reorder_w_q_up(w):
    w3 = w.reshape(MLA_Q_RANK, MLA_HEADS, MLA_QK_DIM)
    nope = w3[:, :, :MLA_NOPE_DIM].reshape(MLA_Q_RANK, MLA_HEADS * MLA_NOPE_DIM)
    rope = w3[:, :, MLA_NOPE_DIM:]
    rope = jnp.concatenate([rope, rope], axis=-1).reshape(MLA_Q_RANK, MLA_HEADS * LANES)
    return jnp.concatenate([nope, rope], axis=1).astype(BF16)


def _reorder_w_kv_up(w):
    w3 = w.reshape(MLA_KV_RANK, MLA_HEADS, MLA_NOPE_DIM + MLA_V_DIM)
    k_nope = w3[:, :, :MLA_NOPE_DIM].reshape(MLA_KV_RANK, MLA_HEADS * MLA_NOPE_DIM)
    v = w3[:, :, MLA_NOPE_DIM:].reshape(MLA_KV_RANK, MLA_HEADS * MLA_V_DIM)
    return jnp.concatenate([k_nope, v], axis=1).astype(BF16)


def _cumsum_matrix(blk):
    j = jnp.arange(blk)[:, None]
    s = jnp.arange(blk)[None, :]
    tri = (j > s).astype(BF16)
    return jnp.concatenate([tri, tri], axis=0)


def _layer(x2d, p2d, pos, invf, batch, seq, g_pre_mix, w_in, g_cq, g_ckv, w_q_up, w_kv_up,
           w_sb_o, w_mla_o, w_out, g_post_mix, g_pre_mlp, w_up, w_down, g_post_mlp,
           w_ple, g_ple, w_ple_gate):
    row = lambda g: g.reshape(1, -1).astype(F32)
    proj, kr = _in_proj(x2d, row(g_pre_mix), *_reorder_w_in(w_in))
    o_sb = _sb_attn(proj, _cumsum_matrix(min(SB_BLK, seq)), batch, seq)
    q, k, v = _mla_prep(proj, kr, pos, invf, row(g_cq), row(g_ckv),
                        _reorder_w_q_up(w_q_up), _reorder_w_kv_up(w_kv_up))
    o_mla, (w_sb_o, w_mla_o, w_out, w_up, w_down, w_ple_gate) = _mla_attn(
        q, k, v, batch, seq, (w_sb_o, w_mla_o, w_out, w_up, w_down, w_ple_gate))
    x1, h2 = _mix(x2d, o_sb, o_mla, proj, w_sb_o, w_mla_o, w_out, row(g_post_mix), row(g_pre_mlp))
    r = _mlp(h2, w_up, w_down, row(g_post_mlp))
    return _ple(x1, r, p2d, w_ple.astype(BF16), w_ple_gate, row(g_ple))


def kernel(x, p, positions, g_pre_mix, w_in, g_cq, g_ckv, w_q_up, w_kv_up, w_sb_o, w_mla_o, w_out, g_post_mix, g_pre_mlp, w_up, w_down, g_post_mlp, w_ple, g_ple, w_ple_gate):
    batch, seq, d = x.shape
    assert d == D_MODEL and seq % 256 == 0, "sequence must tile into stick-breaking blocks"
    t = batch * seq
    half = MLA_ROPE_DIM // 2
    inv_freq = ROPE_THETA ** (-jnp.arange(half, dtype=F32) / half)
    invf = jnp.tile(inv_freq, LANES // half).reshape(1, LANES)
    pos = positions.reshape(t, 1).astype(jnp.int32)
    x2d = x.reshape(t, d)
    for i in range(p.shape[0]):
        x2d = _layer(x2d, p[i].reshape(t, PLE_DIM), pos, invf, batch, seq, g_pre_mix[i], w_in[i],
                     g_cq[i], g_ckv[i], w_q_up[i], w_kv_up[i], w_sb_o[i], w_mla_o[i], w_out[i],
                     g_post_mix[i], g_pre_mlp[i], w_up[i], w_down[i], g_post_mlp[i],
                     w_ple[i], g_ple[i], w_ple_gate[i])
    return x2d.reshape(batch, seq, d)
```

```python
import functools

import jax
import jax.numpy as jnp
from jax import lax
from jax.experimental import pallas as pl
from jax.experimental.pallas import tpu as pltpu

F32 = jnp.float32
BF16 = jnp.bfloat16

D_MODEL = 2048
PLE_DIM = 256
SB_HEADS = 8
SB_HEAD_DIM = 128
MLA_HEADS = 8
MLA_NOPE_DIM = 128
MLA_ROPE_DIM = 64
MLA_V_DIM = 128
MLA_Q_RANK = 512
MLA_KV_RANK = 512
D_FF = 4 * D_MODEL
ROPE_THETA = 10000.0
EPS = 1e-6
SB_WIDTH = SB_HEADS * SB_HEAD_DIM
MLA_WIDTH = MLA_HEADS * MLA_V_DIM
MLA_QK_DIM = MLA_NOPE_DIM + MLA_ROPE_DIM
MLA_QK_PAD = 256

LANES = 128
BF16_SUBLANES = 16
MIB = 1024 * 1024

BLK_SB_Q = 0
BLK_SB_K = 8
BLK_SB_V = 16
BLK_CQ = 24
BLK_CKV = 28
BLK_GATE_SB = 32
BLK_GATE_MLA = 48

IN_TILE_BLOCKS = 16
ROWS_IN_PROJ = 1024
ROWS_MLA_PREP = 512
ROWS_MIX = 512
ROWS_MLP = 1024
MLP_TF = 1024
ROWS_PLE = 512
SB_BLK = 256
SB_HEADS_PER_STEP = 4
MLA_BLK = 512
MLA_ROW_BLOCKS = 4
MLA_HEADS_PER_STEP = 2
VMEM_MIB = dict(w_in_prep=40, in_proj=58, sb_attn=48, mla_prep=48, mla_attn=56, mix=56, mlp=58,
                ple=48)

LOG2E = 1.4426950408889634

SB_LOG2_UNDERFLOW = -104.0 * LOG2E


def _params(semantics, vmem_mib):
    return pltpu.CompilerParams(dimension_semantics=semantics, vmem_limit_bytes=vmem_mib * MIB)


def _rms(xf, g):
    y = xf * lax.rsqrt(jnp.mean(xf * xf, axis=-1, keepdims=True) + EPS)
    return y * g


def _sigmoid(x):
    return 1.0 / (1.0 + jnp.exp(-x))


def _cat_blocks(ref, n):
    return jnp.concatenate([ref[c] for c in range(n)], axis=1)


def _in_proj_kernel(x_ref, g_ref, w_ref, wkr_ref, o_ref, kr_ref, h_ref, *, nblk):
    nt = (((1,), (1,)), ((), ()))

    @pl.when(pl.program_id(1) == 0)
    def _():
        h = _rms(x_ref[...], g_ref[...]).astype(BF16)
        h_ref[...] = h
        kr_ref[...] = lax.dot_general(h, wkr_ref[...], nt, preferred_element_type=F32).astype(BF16)

    acc = lax.dot_general(h_ref[...], w_ref[...], nt, preferred_element_type=F32)
    for c in range(nblk):
        o_ref[c] = acc[:, c * LANES:(c + 1) * LANES].astype(BF16)


def _in_proj(x2d, g, w, w_kr):
    t, d = x2d.shape
    n = w.shape[0]
    tn = IN_TILE_BLOCKS * LANES
    tm = min(ROWS_IN_PROJ, t)
    return pl.pallas_call(
        functools.partial(_in_proj_kernel, nblk=IN_TILE_BLOCKS),
        grid=(t // tm, n // tn),
        in_specs=[
            pl.BlockSpec((tm, d), lambda i, j: (i, 0)),
            pl.BlockSpec((1, d), lambda i, j: (0, 0)),
            pl.BlockSpec((tn, d), lambda i, j: (j, 0)),
            pl.BlockSpec((LANES, d), lambda i, j: (0, 0)),
        ],
        out_specs=[
            pl.BlockSpec((IN_TILE_BLOCKS, tm, LANES), lambda i, j: (j, i, 0)),
            pl.BlockSpec((tm, LANES), lambda i, j: (i, 0)),
        ],
        out_shape=[
            jax.ShapeDtypeStruct((n // LANES, t, LANES), BF16),
            jax.ShapeDtypeStruct((t, LANES), BF16),
        ],
        scratch_shapes=[pltpu.VMEM((tm, d), BF16)],
        compiler_params=_params(("parallel", "arbitrary"), VMEM_MIB["in_proj"]),
        name="in_proj",
    )(x2d, g, w, w_kr)


def _sb_chunk(q, k, u, mask, blk):
    z = lax.dot_general(q, k, (((1,), (1,)), ((), ())), preferred_element_type=F32)
    sign_bit = jnp.uint32(0x80000000)
    neg_abs = lax.bitcast_convert_type(lax.bitcast_convert_type(z, jnp.uint32) | sign_bit, F32)
    c = jnp.log(1.0 + jnp.exp2(neg_abs)) * LOG2E
    neg_part = jnp.minimum(z, 0.0)
    log_beta = neg_part - c
    log_fail = (neg_abs - neg_part) - c
    if mask is not None:
        log_fail = jnp.where(mask, log_fail, 0.0)
    hi = log_fail.astype(BF16)
    lo = (log_fail - hi.astype(F32)).astype(BF16)
    later = jnp.dot(jnp.concatenate([hi, lo], axis=1), u, preferred_element_type=F32)
    return log_beta + later, jnp.sum(log_fail, axis=1, keepdims=True)


def _sb_kernel(q_ref, k_ref, v_ref, u_ref, o_ref, acc_ref, carry_ref, *, blk, heads):
    qi = pl.program_id(2)
    u = u_ref[...]

    def wide(c):
        return jnp.concatenate([c] * (blk // LANES), axis=1)

    def kv(g, kb):
        start = pl.multiple_of(kb * blk, blk)
        return k_ref[g, pl.ds(start, blk), :], v_ref[g, pl.ds(start, blk), :]

    def pv(w, v):
        return jnp.dot(w.astype(BF16), v, preferred_element_type=F32)

    row = lax.broadcasted_iota(jnp.int32, (blk, blk), 0)
    col = lax.broadcasted_iota(jnp.int32, (blk, blk), 1)
    mask = col < row
    no_prev = jnp.where(qi > 0, 0.0, -jnp.inf)
    remaining = []
    for g in range(heads):
        q = q_ref[g]
        k_a, v_a = kv(g, qi)
        k_b, v_b = kv(g, jnp.maximum(qi - 1, 0))
        logw_a, total_a = _sb_chunk(q, k_a, u, mask, blk)
        logw_b, total_b = _sb_chunk(q, k_b, u, None, blk)
        w_a = jnp.where(mask, jnp.exp2(logw_a), 0.0)
        w_b = jnp.exp2(logw_b + (total_a + no_prev))
        acc_ref[g] = pv(w_a, v_a) + pv(w_b, v_b)
        carry = total_a + total_b
        carry_ref[g] = jnp.broadcast_to(carry, (blk, LANES))
        remaining.append(jnp.max(carry))

    for g in range(heads):
        q = q_ref[g]

        def body(state, g=g, q=q):
            kb, _ = state
            k, v = kv(g, kb)
            logw, total = _sb_chunk(q, k, u, None, blk)
            carry = carry_ref[g]
            acc_ref[g] += pv(jnp.exp2(logw + wide(carry)), v)
            carry = carry + total
            carry_ref[g] = carry
            return kb - 1, jnp.max(carry)

        def cond(state):
            kb, left = state
            return jnp.logical_and(kb >= 0, left > SB_LOG2_UNDERFLOW)

        lax.while_loop(cond, body, (qi - 2, remaining[g]))
        o_ref[g] = acc_ref[g].astype(BF16)


def _sb_attn(proj, cum_mat, batch, seq):
    blk = min(SB_BLK, seq)
    heads = SB_HEADS_PER_STEP
    nq = seq // blk
    t = batch * seq
    return pl.pallas_call(
        functools.partial(_sb_kernel, blk=blk, heads=heads),
        grid=(batch, SB_HEADS // heads, nq),
        in_specs=[
            pl.BlockSpec((heads, blk, LANES), lambda b, h, i: (BLK_SB_Q // heads + h, b * nq + i, 0)),
            pl.BlockSpec((heads, seq, LANES), lambda b, h, i: (BLK_SB_K // heads + h, b, 0)),
            pl.BlockSpec((heads, seq, LANES), lambda b, h, i: (BLK_SB_V // heads + h, b, 0)),
            pl.BlockSpec((2 * blk, blk), lambda b, h, i: (0, 0)),
        ],
        out_specs=pl.BlockSpec((heads, blk, LANES), lambda b, h, i: (h, b * nq + i, 0)),
        out_shape=jax.ShapeDtypeStruct((SB_HEADS, t, LANES), BF16),
        scratch_shapes=[pltpu.VMEM((heads, blk, LANES), F32), pltpu.VMEM((heads, blk, LANES), F32)],
        compiler_params=_params(("parallel", "parallel", "arbitrary"), VMEM_MIB["sb_attn"]),
        name="sb_attn",
    )(proj, proj, proj, cum_mat)


def _mla_prep_kernel(cq_ref, ckv_ref, kr_ref, pos_ref, invf_ref, gq_ref, gkv_ref, wq_ref, wkv_ref,
                     q_out, k_out, v_out):
    nlat = MLA_Q_RANK // LANES
    cq = _cat_blocks(cq_ref, nlat).astype(F32)
    ckv = _cat_blocks(ckv_ref, nlat).astype(F32)
    q = jnp.dot(_rms(cq, gq_ref[...]).astype(BF16), wq_ref[...], preferred_element_type=F32)
    kv = jnp.dot(_rms(ckv, gkv_ref[...]).astype(BF16), wkv_ref[...], preferred_element_type=F32)

    half = MLA_ROPE_DIM // 2
    groups = LANES // half
    rows = pos_ref.shape[0] // groups
    lane = lax.broadcasted_iota(jnp.int32, (rows, LANES), 1)
    pos = pos_ref[...].astype(F32)
    packed = pos[(groups - 1) * rows:]
    for j in range(groups - 2, -1, -1):
        packed = jnp.where(lane < (j + 1) * half, pos[j * rows:(j + 1) * rows], packed)
    ang = packed * invf_ref[...]
    cos_p, sin_p = jnp.cos(ang), jnp.sin(ang)

    def unpack(t, j, k):
        shift = ((k - j) * half) % LANES
        return pltpu.roll(t, shift, axis=1) if shift else t

    cos_t, sin_t = [], []
    for j in range(groups):
        cos_t.append(jnp.where(lane < half, unpack(cos_p, j, 0),
                               jnp.where(lane < MLA_ROPE_DIM, unpack(cos_p, j, 1), 0.0)))
        sin_t.append(jnp.where(lane < half, -unpack(sin_p, j, 0),
                               jnp.where(lane < MLA_ROPE_DIM, unpack(sin_p, j, 1), 0.0)))
    cos_t = jnp.concatenate(cos_t, axis=0)
    sin_t = jnp.concatenate(sin_t, axis=0)

    def rope(a):
        return a * cos_t + pltpu.roll(a, half, axis=1) * sin_t

    k_rope = rope(kr_ref[...].astype(F32))
    scale = MLA_QK_DIM ** -0.5 * LOG2E
    for h in range(MLA_HEADS):
        lo, hi = h * LANES, (h + 1) * LANES
        q_nope = q[:, lo:hi] * scale
        q_rope = rope(q[:, MLA_HEADS * LANES + lo:MLA_HEADS * LANES + hi]) * scale
        q_out[h] = jnp.concatenate([q_nope, q_rope], axis=1).astype(BF16)
        k_out[h] = jnp.concatenate([kv[:, lo:hi], k_rope], axis=1).astype(BF16)
        v_out[h] = kv[:, MLA_HEADS * LANES + lo:MLA_HEADS * LANES + hi].astype(BF16)


def _mla_prep(proj, kr, pos, invf, g_cq, g_ckv, wq, wkv):
    t = proj.shape[1]
    tm = min(ROWS_MLA_PREP, t)
    nlat = MLA_Q_RANK // LANES
    const = lambda i: (0, 0)
    return pl.pallas_call(
        _mla_prep_kernel,
        grid=(t // tm,),
        in_specs=[
            pl.BlockSpec((nlat, tm, LANES), lambda i: (BLK_CQ // nlat, i, 0)),
            pl.BlockSpec((nlat, tm, LANES), lambda i: (BLK_CKV // nlat, i, 0)),
            pl.BlockSpec((tm, LANES), lambda i: (i, 0)),
            pl.BlockSpec((tm, 1), lambda i: (i, 0)),
            pl.BlockSpec((1, LANES), const),
            pl.BlockSpec((1, MLA_Q_RANK), const),
            pl.BlockSpec((1, MLA_KV_RANK), const),
            pl.BlockSpec(wq.shape, const),
            pl.BlockSpec(wkv.shape, const),
        ],
        out_specs=[
            pl.BlockSpec((MLA_HEADS, tm, MLA_QK_PAD), lambda i: (0, i, 0)),
            pl.BlockSpec((MLA_HEADS, tm, MLA_QK_PAD), lambda i: (0, i, 0)),
            pl.BlockSpec((MLA_HEADS, tm, LANES), lambda i: (0, i, 0)),
        ],
        out_shape=[
            jax.ShapeDtypeStruct((MLA_HEADS, t, MLA_QK_PAD), BF16),
            jax.ShapeDtypeStruct((MLA_HEADS, t, MLA_QK_PAD), BF16),
            jax.ShapeDtypeStruct((MLA_HEADS, t, LANES), BF16),
        ],
        compiler_params=_params(("parallel",), VMEM_MIB["mla_prep"]),
        name="mla_prep",
    )(proj, proj, kr, pos, invf, g_cq, g_ckv, wq, wkv)


def _mla_kernel(*refs, blk, heads, subs, ncast):
    q_ref, k_ref, v_ref = refs[:3]
    o_ref = refs[3 + ncast]
    m_ref, acc_ref = refs[-2:]
    for src, dst in zip(refs[3:3 + ncast], refs[4 + ncast:4 + 2 * ncast]):
        dst[...] = src[...].astype(BF16)
    qi = pl.program_id(2)
    m_ref[...] = jnp.full_like(m_ref, -jnp.inf)
    acc_ref[...] = jnp.zeros_like(acc_ref)

    def step(kb, kinds):
        start = pl.multiple_of(kb * blk, blk)
        for g in range(heads):
            k = k_ref[g, pl.ds(start, blk), :]
            v = v_ref[g, pl.ds(start, blk), :]
            v_ext = jnp.concatenate([v, jnp.ones_like(v)], axis=1)
            for r, kind in enumerate(kinds):
                if kind is None:
                    continue
                rows = slice(r * blk, (r + 1) * blk)
                s = lax.dot_general(q_ref[g, rows, :], k, (((1,), (1,)), ((), ())),
                                    preferred_element_type=F32)
                if kind == "diag":
                    row = lax.broadcasted_iota(jnp.int32, (blk, blk), 0)
                    col = lax.broadcasted_iota(jnp.int32, (blk, blk), 1)
                    s = jnp.where(col <= row, s, -jnp.inf)
                m_prev = m_ref[g, rows, :]
                m_new = jnp.maximum(m_prev, jnp.max(s, axis=1, keepdims=True))
                alpha = jnp.exp2(m_prev - m_new)
                p = jnp.exp2(s - jnp.concatenate([m_new] * (blk // LANES), axis=1))
                acc_ref[g, rows, :] = (jnp.concatenate([alpha, alpha], axis=1) * acc_ref[g, rows, :]
                                       + jnp.dot(p.astype(BF16), v_ext, preferred_element_type=F32))
                m_ref[g, rows, :] = m_new

    def below_diagonal(j, c):
        for d in range(subs):
            step(subs * j + d, ("full",) * subs)
        return c

    lax.fori_loop(0, qi, below_diagonal, 0)
    for d in range(subs):
        step(subs * qi + d, tuple("diag" if r == d else ("full" if r > d else None)
                                  for r in range(subs)))
    for g in range(heads):
        acc = acc_ref[g]
        o_ref[g] = (acc[:, :LANES] / acc[:, LANES:]).astype(BF16)


def _mla_attn(q, k, v, batch, seq, weights):
    blk = min(MLA_BLK, seq)
    subs = min(MLA_ROW_BLOCKS, seq // blk)
    heads = MLA_HEADS_PER_STEP
    tq = subs * blk
    assert seq % tq == 0, "sequence must tile into MLA query tiles"
    nq = seq // tq
    ngroups = MLA_HEADS // heads
    t = batch * seq
    steps = batch * ngroups * nq
    assert all(w.shape[0] % (steps * BF16_SUBLANES) == 0 for w in weights)
    slab = lambda b, h, i: ((b * ngroups + h) * nq + i, 0)
    slabs = [pl.BlockSpec((w.shape[0] // steps, w.shape[1]), slab) for w in weights]
    out = pl.pallas_call(
        functools.partial(_mla_kernel, blk=blk, heads=heads, subs=subs, ncast=len(weights)),
        grid=(batch, ngroups, nq),
        in_specs=[
            pl.BlockSpec((heads, tq, MLA_QK_PAD), lambda b, h, i: (h, b * nq + i, 0)),
            pl.BlockSpec((heads, seq, MLA_QK_PAD), lambda b, h, i: (h, b, 0)),
            pl.BlockSpec((heads, seq, LANES), lambda b, h, i: (h, b, 0)),
        ] + slabs,
        out_specs=[pl.BlockSpec((heads, tq, LANES), lambda b, h, i: (h, b * nq + i, 0))] + slabs,
        out_shape=[jax.ShapeDtypeStruct((MLA_HEADS, t, LANES), BF16)]
        + [jax.ShapeDtypeStruct(w.shape, BF16) for w in weights],
        scratch_shapes=[pltpu.VMEM((heads, tq, LANES), F32),
                        pltpu.VMEM((heads, tq, 2 * LANES), F32)],
        compiler_params=_params(("parallel", "parallel", "arbitrary"), VMEM_MIB["mla_attn"]),
        name="mla_attn",
    )(q, k, v, *weights)
    return out[0], out[1:]


def _mix_kernel(x_ref, osb_ref, omla_ref, gsb_ref, gmla_ref, wsb_ref, wmla_ref, wout_ref,
                gpost_ref, gpre_ref, x1_ref, h2_ref):
    nd = D_MODEL // LANES
    tm = x_ref.shape[0]
    half = tm // 2
    for rows in (slice(0, half), slice(half, tm)):
        cat = lambda ref, n: jnp.concatenate([ref[c, rows, :] for c in range(n)], axis=1)
        a = jnp.dot(cat(osb_ref, SB_HEADS), wsb_ref[...], preferred_element_type=F32)
        b = jnp.dot(cat(omla_ref, MLA_HEADS), wmla_ref[...], preferred_element_type=F32)
        gate_sb = cat(gsb_ref, nd).astype(F32)
        gate_mla = cat(gmla_ref, nd).astype(F32)
        mixed = _sigmoid(gate_sb) * a + _sigmoid(gate_mla) * b
        y = jnp.dot(mixed.astype(BF16), wout_ref[...], preferred_element_type=F32)
        x1 = x_ref[rows, :] + _rms(y, gpost_ref[...])
        x1_ref[rows, :] = x1
        h2_ref[rows, :] = _rms(x1, gpre_ref[...]).astype(BF16)


def _mix(x2d, o_sb, o_mla, proj, w_sb_o, w_mla_o, w_out, g_post, g_pre):
    t, d = x2d.shape
    tm = min(ROWS_MIX, t)
    nd = d // LANES
    const = lambda i: (0, 0)
    single = pl.Buffered(1)
    return pl.pallas_call(
        _mix_kernel,
        grid=(t // tm,),
        in_specs=[
            pl.BlockSpec((tm, d), lambda i: (i, 0)),
            pl.BlockSpec((SB_HEADS, tm, LANES), lambda i: (0, i, 0)),
            pl.BlockSpec((MLA_HEADS, tm, LANES), lambda i: (0, i, 0)),
            pl.BlockSpec((nd, tm, LANES), lambda i: (BLK_GATE_SB // nd, i, 0)),
            pl.BlockSpec((nd, tm, LANES), lambda i: (BLK_GATE_MLA // nd, i, 0)),
            pl.BlockSpec(w_sb_o.shape, const, pipeline_mode=single),
            pl.BlockSpec(w_mla_o.shape, const, pipeline_mode=single),
            pl.BlockSpec(w_out.shape, const, pipeline_mode=single),
            pl.BlockSpec((1, d), const),
            pl.BlockSpec((1, d), const),
        ],
        out_specs=[
            pl.BlockSpec((tm, d), lambda i: (i, 0)),
            pl.BlockSpec((tm, d), lambda i: (i, 0)),
        ],
        out_shape=[
            jax.ShapeDtypeStruct((t, d), F32),
            jax.ShapeDtypeStruct((t, d), BF16),
        ],
        compiler_params=_params(("parallel",), VMEM_MIB["mix"]),
        name="mix",
    )(x2d, o_sb, o_mla, proj, proj, w_sb_o, w_mla_o, w_out, g_post, g_pre)


def _mlp_kernel(h_ref, wup_ref, wdown_ref, g_ref, o_ref):
    f = pl.program_id(1)

    @pl.when(f == 0)
    def _():
        o_ref[...] = jnp.zeros_like(o_ref)

    u = jnp.dot(h_ref[...], wup_ref[...], preferred_element_type=F32)
    u = jnp.square(jnp.maximum(u, 0.0))
    o_ref[...] += jnp.dot(u.astype(BF16), wdown_ref[...], preferred_element_type=F32)

    @pl.when(f == pl.num_programs(1) - 1)
    def _():
        o_ref[...] = _rms(o_ref[...], g_ref[...])


def _mlp(h2, w_up, w_down, g_post):
    t, d = h2.shape
    dff = w_up.shape[1]
    tf = MLP_TF
    tm = min(ROWS_MLP, t)
    return pl.pallas_call(
        _mlp_kernel,
        grid=(t // tm, dff // tf),
        in_specs=[
            pl.BlockSpec((tm, d), lambda i, f: (i, 0)),
            pl.BlockSpec((d, tf), lambda i, f: (0, f)),
            pl.BlockSpec((tf, d), lambda i, f: (f, 0)),
            pl.BlockSpec((1, d), lambda i, f: (0, 0)),
        ],
        out_specs=pl.BlockSpec((tm, d), lambda i, f: (i, 0)),
        out_shape=jax.ShapeDtypeStruct((t, d), F32),
        compiler_params=_params(("parallel", "arbitrary"), VMEM_MIB["mlp"]),
        name="mlp",
    )(h2, w_up, w_down, g_post)


def _ple_kernel(x1_ref, r_ref, p_ref, wple_ref, wgate_ref, g_ref, o_ref):
    x = x1_ref[...] + r_ref[...]
    e = _rms(jnp.dot(p_ref[...].astype(BF16), wple_ref[...], preferred_element_type=F32), g_ref[...])
    gate = _sigmoid(jnp.dot(x.astype(BF16), wgate_ref[...], preferred_element_type=F32))
    o_ref[...] = x + gate * e


def _ple(x1, r, p2d, w_ple, w_gate, g_ple):
    t, d = x1.shape
    tm = min(ROWS_PLE, t)
    const = lambda i: (0, 0)
    single = pl.Buffered(1)
    return pl.pallas_call(
        _ple_kernel,
        grid=(t // tm,),
        in_specs=[
            pl.BlockSpec((tm, d), lambda i: (i, 0)),
            pl.BlockSpec((tm, d), lambda i: (i, 0)),
            pl.BlockSpec((tm, p2d.shape[1]), lambda i: (i, 0)),
            pl.BlockSpec(w_ple.shape, const, pipeline_mode=single),
            pl.BlockSpec(w_gate.shape, const, pipeline_mode=single),
            pl.BlockSpec((1, d), const),
        ],
        out_specs=pl.BlockSpec((tm, d), lambda i: (i, 0)),
        out_shape=jax.ShapeDtypeStruct((t, d), F32),
        compiler_params=_params(("parallel",), VMEM_MIB["ple"]),
        name="ple",
    )(x1, r, p2d, w_ple, w_gate, g_ple)


def _w_in_prep_kernel(a_ref, b_ref, rope_ref, main_ref, kr_ref, *, first_gate_block):
    b = pl.program_id(0)
    a = a_ref[...]

    @pl.when(b == 0)
    def _():
        main_ref[...] = (a * (SB_HEAD_DIM ** -0.5 * LOG2E)).astype(BF16)
        rope = rope_ref[...].astype(BF16)
        kr_ref[...] = jnp.concatenate([rope, rope], axis=0)

    @pl.when(jnp.logical_and(b > 0, b < first_gate_block))
    def _():
        main_ref[...] = a.astype(BF16)

    @pl.when(b >= first_gate_block)
    def _():
        main_ref[...] = jnp.concatenate([a[MLA_ROPE_DIM:], b_ref[...]], axis=0).astype(BF16)


def _reorder_w_in(w):
    wt = jnp.swapaxes(w, 1, 2)
    _, n, d = wt.shape
    n_main = n - MLA_ROPE_DIM
    o_kr = 3 * SB_WIDTH + MLA_Q_RANK + MLA_KV_RANK
    r = SB_WIDTH
    assert o_kr % r == 0 and n_main % r == 0 and r % MLA_ROPE_DIM == 0
    per = r // MLA_ROPE_DIM
    return pl.pallas_call(
        functools.partial(_w_in_prep_kernel, first_gate_block=o_kr // r),
        grid=(n_main // r,),
        in_specs=[pl.BlockSpec((None, r, d), lambda b: (0, b, 0)),
                  pl.BlockSpec((None, MLA_ROPE_DIM, d), lambda b: (0, (b + 1) * per, 0)),
                  pl.BlockSpec((None, MLA_ROPE_DIM, d), lambda b: (0, o_kr // MLA_ROPE_DIM, 0))],
        out_specs=[pl.BlockSpec((r, d), lambda b: (b, 0)),
                   pl.BlockSpec((2 * MLA_ROPE_DIM, d), lambda b: (0, 0))],
        out_shape=[jax.ShapeDtypeStruct((n_main, d), BF16),
                   jax.ShapeDtypeStruct((2 * MLA_ROPE_DIM, d), BF16)],
        compiler_params=_params(("arbitrary",), VMEM_MIB["w_in_prep"]),
        name="w_in_prep",
    )(wt, wt, wt)


def _reorder_w_q_up(w):
    w3 = w.reshape(MLA_Q_RANK, MLA_HEADS, MLA_QK_DIM)
    nope = w3[:, :, :MLA_NOPE_DIM].reshape(MLA_Q_RANK, MLA_HEADS * MLA_NOPE_DIM)
    rope = w3[:, :, MLA_NOPE_DIM:]
    rope = jnp.concatenate([rope, rope], axis=-1).reshape(MLA_Q_RANK, MLA_HEADS * LANES)
    return jnp.concatenate([nope, rope], axis=1).astype(BF16)


def _reorder_w_kv_up(w):
    w3 = w.reshape(MLA_KV_RANK, MLA_HEADS, MLA_NOPE_DIM + MLA_V_DIM)
    k_nope = w3[:, :, :MLA_NOPE_DIM].reshape(MLA_KV_RANK, MLA_HEADS * MLA_NOPE_DIM)
    v = w3[:, :, MLA_NOPE_DIM:].reshape(MLA_KV_RANK, MLA_HEADS * MLA_V_DIM)
    return jnp.concatenate([k_nope, v], axis=1).astype(BF16)


def _cumsum_matrix(blk):
    j = jnp.arange(blk)[:, None]
    s = jnp.arange(blk)[None, :]
    tri = (j > s).astype(BF16)
    return jnp.concatenate([tri, tri], axis=0)


def _layer(x2d, p2d, pos, invf, batch, seq, g_pre_mix, w_in, g_cq, g_ckv, w_q_up, w_kv_up,
           w_sb_o, w_mla_o, w_out, g_post_mix, g_pre_mlp, w_up, w_down, g_post_mlp,
           w_ple, g_ple, w_ple_gate):
    row = lambda g: g.reshape(1, -1).astype(F32)
    proj, kr = _in_proj(x2d, row(g_pre_mix), *_reorder_w_in(w_in))
    o_sb = _sb_attn(proj, _cumsum_matrix(min(SB_BLK, seq)), batch, seq)
    q, k, v = _mla_prep(proj, kr, pos, invf, row(g_cq), row(g_ckv),
                        _reorder_w_q_up(w_q_up), _reorder_w_kv_up(w_kv_up))
    o_mla, (w_sb_o, w_mla_o, w_out, w_up, w_down, w_ple_gate) = _mla_attn(
        q, k, v, batch, seq, (w_sb_o, w_mla_o, w_out, w_up, w_down, w_ple_gate))
    x1, h2 = _mix(x2d, o_sb, o_mla, proj, w_sb_o, w_mla_o, w_out, row(g_post_mix), row(g_pre_mlp))
    r = _mlp(h2, w_up, w_down, row(g_post_mlp))
    return _ple(x1, r, p2d, w_ple.astype(BF16), w_ple_gate, row(g_ple))


def kernel(x, p, positions, g_pre_mix, w_in, g_cq, g_ckv, w_q_up, w_kv_up, w_sb_o, w_mla_o, w_out, g_post_mix, g_pre_mlp, w_up, w_down, g_post_mlp, w_ple, g_ple, w_ple_gate):
    batch, seq, d = x.shape
    assert d == D_MODEL and seq % 256 == 0, "sequence must tile into stick-breaking blocks"
    t = batch * seq
    half = MLA_ROPE_DIM // 2
    inv_freq = ROPE_THETA ** (-jnp.arange(half, dtype=F32) / half)
    invf = jnp.tile(inv_freq, LANES // half).reshape(1, LANES)
    pos = positions.reshape(t, 1).astype(jnp.int32)
    x2d = x.reshape(t, d)
    for i in range(p.shape[0]):
        x2d = _layer(x2d, p[i].reshape(t, PLE_DIM), pos, invf, batch, seq, g_pre_mix[i], w_in[i:i + 1],
                     g_cq[i], g_ckv[i], w_q_up[i], w_kv_up[i], w_sb_o[i], w_mla_o[i], w_out[i],
                     g_post_mix[i], g_pre_mlp[i], w_up[i], w_down[i], g_post_mlp[i],
                     w_ple[i], g_ple[i], w_ple_gate[i])
    return x2d.reshape(batch, seq, d)
```
